```python
import jax, jax.numpy as jnp
from jax import lax
import numpy as np

D_MODEL = 1024
BATCH = 8
SEQ = 4096
DEPTH = 2

N_META = 16
CHUNK = 128
PAD_FRONT = (-N_META) % CHUNK
RET_HEADS = 4
RET_QK_DIM = D_MODEL // 8
RET_V_DIM = 2 * RET_QK_DIM
RET_QK_W = RET_HEADS * RET_QK_DIM
RET_V_W = RET_HEADS * RET_V_DIM
CONV_CH = D_MODEL
CONV_WIDTH = 31
MIX_IN_EVEN = 2 * RET_QK_W + 2 * RET_V_W + 2 * CONV_CH
MIX_OUT_EVEN = RET_V_W + CONV_CH
RET_DECAY_OFFSET = 5.0
ROPE_BASE = 10000.0
SB_HEADS = 16
SB_HEAD_DIM = D_MODEL // SB_HEADS
D_FF = 4 * D_MODEL
EPS = 1e-6
N_EVEN = (DEPTH + 1) // 2
N_ODD = DEPTH // 2

kernel_name = "hybrid_retention_conformer_stickbreaking_trunk"


def rmsnorm(x, g):
    xf = x.astype(jnp.float32)
    y = xf * lax.rsqrt(jnp.mean(xf * xf, axis=-1, keepdims=True) + EPS) * g.astype(jnp.float32)
    return y.astype(x.dtype)


def layernorm(x, g, b):
    xf = x.astype(jnp.float32)
    mu = jnp.mean(xf, axis=-1, keepdims=True)
    var = jnp.mean(jnp.square(xf - mu), axis=-1, keepdims=True)
    y = (xf - mu) * lax.rsqrt(var + EPS) * g.astype(jnp.float32) + b.astype(jnp.float32)
    return y.astype(x.dtype)


def rotary(x):
    P, d = x.shape[1], x.shape[-1]
    half = d // 2
    inv_freq = ROPE_BASE ** (-jnp.arange(half, dtype=jnp.float32) / half)
    ang = jnp.arange(P, dtype=jnp.float32)[:, None] * inv_freq[None, :]
    cos = jnp.cos(ang)[None, :, None, :]
    sin = jnp.sin(ang)[None, :, None, :]
    x1, x2 = x[..., :half], x[..., half:]
    return jnp.concatenate([x1 * cos - x2 * sin, x1 * sin + x2 * cos], axis=-1)


def retention_chunkwise(q, k, v):
    b, P, H, dk = q.shape
    dv = v.shape[-1]
    n = P // CHUNK
    log_g = jnp.log1p(-jnp.exp2(-RET_DECAY_OFFSET - jnp.arange(H, dtype=jnp.float32)))
    idx = jnp.arange(CHUNK, dtype=jnp.float32)
    diff = idx[:, None] - idx[None, :]
    inner_decay = jnp.where(diff[None] >= 0, jnp.exp(jnp.maximum(diff, 0.0)[None] * log_g[:, None, None]), 0.0)
    qc = q.reshape(b, n, CHUNK, H, dk)
    kc = k.reshape(b, n, CHUNK, H, dk)
    vc = v.reshape(b, n, CHUNK, H, dv)
    scores = jnp.einsum('bnihd,bnjhd->bnhij', qc, kc) * inner_decay
    o_inner = jnp.einsum('bnhij,bnjhe->bnihe', scores, vc)
    k_dec = kc * jnp.exp((CHUNK - 1 - idx)[:, None] * log_g[None, :])[:, :, None]
    kv = jnp.einsum('bnjhd,bnjhe->nbhde', k_dec, vc)
    chunk_decay = jnp.exp(CHUNK * log_g)[None, :, None, None]

    def step(state, kv_n):
        return chunk_decay * state + kv_n, state

    _, prev = lax.scan(step, jnp.zeros((b, H, dk, dv), jnp.float32), kv)
    q_dec = qc * jnp.exp((idx + 1.0)[:, None] * log_g[None, :])[:, :, None]
    o_cross = jnp.einsum('bnihd,nbhde->bnihe', q_dec, prev)
    return (o_inner + o_cross).reshape(b, P, H, dv)


def head_groupnorm(o, g):
    mu = jnp.mean(o, axis=-1, keepdims=True)
    var = jnp.mean(jnp.square(o - mu), axis=-1, keepdims=True)
    return (o - mu) * lax.rsqrt(var + EPS) * g.astype(jnp.float32)


def conformer_conv(u, conv_w, conv_b, ln_g, ln_b):
    a, gate = jnp.split(u, 2, axis=-1)
    hdn = a * jax.nn.sigmoid(gate)
    y = lax.conv_general_dilated(
        hdn, conv_w[:, None, :].astype(hdn.dtype), window_strides=(1,),
        padding=[(CONV_WIDTH - 1, 0)], dimension_numbers=('NWC', 'WIO', 'NWC'),
        feature_group_count=CONV_CH)
    y = y + conv_b.astype(y.dtype)
    return jax.nn.silu(layernorm(y, ln_g, ln_b))


def even_mixer(h, w_in, gn_g, conv_w, conv_b, ln_g, ln_b, w_out):
    b, L, _ = h.shape
    proj = h @ w_in.astype(h.dtype)
    q, k, v, g, u = jnp.split(proj, [RET_QK_W, 2 * RET_QK_W, 2 * RET_QK_W + RET_V_W,
                                     2 * RET_QK_W + 2 * RET_V_W], axis=-1)
    pad = ((0, 0), (PAD_FRONT, 0), (0, 0), (0, 0))
    q = jnp.pad(q.astype(jnp.float32).reshape(b, L, RET_HEADS, RET_QK_DIM), pad)
    k = jnp.pad(k.astype(jnp.float32).reshape(b, L, RET_HEADS, RET_QK_DIM), pad)
    v = jnp.pad(v.astype(jnp.float32).reshape(b, L, RET_HEADS, RET_V_DIM), pad)
    q = rotary(q)
    k = rotary(k) * (RET_QK_DIM ** -0.5)
    o = retention_chunkwise(q, k, v)[:, PAD_FRONT:]
    o = head_groupnorm(o, gn_g).reshape(b, L, RET_V_W).astype(h.dtype)
    o = jax.nn.silu(g) * o
    c = conformer_conv(u, conv_w, conv_b, ln_g, ln_b)
    return jnp.concatenate([o, c], axis=-1) @ w_out.astype(h.dtype)


def stick_breaking(q, k, v, n_pad):
    b, H, P, d = q.shape
    n = P // CHUNK
    scale = d ** -0.5
    key_pos = jnp.arange(P)

    def block(i):
        qb = lax.dynamic_slice_in_dim(q, i * CHUNK, CHUNK, axis=2)
        z = jnp.einsum('bhqd,bhkd->bhqk', qb, k) * scale
        q_pos = i * CHUNK + jnp.arange(CHUNK)
        valid = (key_pos[None, :] < q_pos[:, None]) & (key_pos[None, :] >= n_pad)
        log_keep = jnp.where(valid, jax.nn.log_sigmoid(-z), 0.0)
        after = lax.cumsum(log_keep, axis=3, reverse=True) - log_keep
        w = jnp.where(valid, jnp.exp(jax.nn.log_sigmoid(z) + after), 0.0)
        return jnp.einsum('bhqk,bhkd->bhqd', w, v)

    out = lax.map(block, jnp.arange(n))
    return jnp.transpose(out, (1, 0, 3, 2, 4)).reshape(b, P, H, d)


def odd_mixer(h, w_qkv, qn_g, kn_g, w_o):
    b, L, _ = h.shape
    qkv = h @ w_qkv.astype(h.dtype)
    q, k, v = jnp.split(qkv, 3, axis=-1)
    q = rmsnorm(q.reshape(b, L, SB_HEADS, SB_HEAD_DIM), qn_g)
    k = rmsnorm(k.reshape(b, L, SB_HEADS, SB_HEAD_DIM), kn_g)
    v = v.reshape(b, L, SB_HEADS, SB_HEAD_DIM)
    pad = ((0, 0), (PAD_FRONT, 0), (0, 0), (0, 0))
    to_bhpd = lambda t: jnp.transpose(jnp.pad(t.astype(jnp.float32), pad), (0, 2, 1, 3))
    o = stick_breaking(to_bhpd(q), to_bhpd(k), to_bhpd(v), PAD_FRONT)[:, PAD_FRONT:]
    o = o.reshape(b, L, D_MODEL).astype(h.dtype)
    return o @ w_o.astype(h.dtype)


def sq_relu_mlp(h, w1, w2):
    return jnp.square(jax.nn.relu(h @ w1.astype(h.dtype))) @ w2.astype(h.dtype)


def setup_inputs(seed: int = 0) -> dict:
    key = jax.random.key(seed)
    ks = jax.random.split(key, 17)
    nrm = lambda kk, shape, s: jax.random.normal(kk, shape, jnp.float32) * s
    return {
        "x": nrm(ks[0], (BATCH, SEQ, D_MODEL), 1.0),
        "meta": nrm(ks[1], (N_META, D_MODEL), 1.0),
        "norm_mix_g": 1.0 + nrm(ks[2], (DEPTH, D_MODEL), 0.02),
        "norm_mlp_g": 1.0 + nrm(ks[3], (DEPTH, D_MODEL), 0.02),
        "even_w_in": nrm(ks[4], (N_EVEN, D_MODEL, MIX_IN_EVEN), D_MODEL ** -0.5),
        "even_ret_gn_g": 1.0 + nrm(ks[5], (N_EVEN, RET_HEADS, RET_V_DIM), 0.02),
        "even_conv_w": nrm(ks[6], (N_EVEN, CONV_WIDTH, CONV_CH), CONV_WIDTH ** -0.5),
        "even_conv_b": nrm(ks[7], (N_EVEN, CONV_CH), 0.01),
        "even_conv_ln_g": 1.0 + nrm(ks[8], (N_EVEN, CONV_CH), 0.02),
        "even_conv_ln_b": nrm(ks[9], (N_EVEN, CONV_CH), 0.01),
        "even_w_out": nrm(ks[10], (N_EVEN, MIX_OUT_EVEN, D_MODEL), MIX_OUT_EVEN ** -0.5),
        "odd_w_qkv": nrm(ks[11], (N_ODD, D_MODEL, 3 * D_MODEL), D_MODEL ** -0.5),
        "odd_q_norm_g": 1.0 + nrm(ks[12], (N_ODD, SB_HEAD_DIM), 0.02),
        "odd_k_norm_g": 1.0 + nrm(ks[13], (N_ODD, SB_HEAD_DIM), 0.02),
        "odd_w_o": nrm(ks[14], (N_ODD, D_MODEL, D_MODEL), D_MODEL ** -0.5),
        "mlp_w1": nrm(ks[15], (DEPTH, D_MODEL, D_FF), D_MODEL ** -0.5),
        "mlp_w2": nrm(ks[16], (DEPTH, D_FF, D_MODEL), D_FF ** -0.5),
    }


def reference(x, meta, norm_mix_g, norm_mlp_g, even_w_in, even_ret_gn_g, even_conv_w,
              even_conv_b, even_conv_ln_g, even_conv_ln_b, even_w_out, odd_w_qkv,
              odd_q_norm_g, odd_k_norm_g, odd_w_o, mlp_w1, mlp_w2):
    b = x.shape[0]
    meta_b = jnp.broadcast_to(meta[None].astype(x.dtype), (b, N_META, D_MODEL))
    h = jnp.concatenate([meta_b, x], axis=1)
    for layer in range(DEPTH):
        j = layer // 2
        hn = rmsnorm(h, norm_mix_g[layer])
        if layer % 2 == 0:
            mix = even_mixer(hn, even_w_in[j], even_ret_gn_g[j], even_conv_w[j], even_conv_b[j],
                             even_conv_ln_g[j], even_conv_ln_b[j], even_w_out[j])
        else:
            mix = odd_mixer(hn, odd_w_qkv[j], odd_q_norm_g[j], odd_k_norm_g[j], odd_w_o[j])
        h = h + mix
        h = h + sq_relu_mlp(rmsnorm(h, norm_mlp_g[layer]), mlp_w1[layer], mlp_w2[layer])
    return h[:, N_META:]
```

```python
import functools

import numpy as np
import jax
import jax.numpy as jnp
from jax import lax
from jax.experimental import pallas as pl
from jax.experimental.pallas import tpu as pltpu

F32 = jnp.float32
BF16 = jnp.bfloat16

D_MODEL = 1024
N_META = 16
CHUNK = 128
PAD_FRONT = (-N_META) % CHUNK
RET_HEADS = 4
RET_QK_DIM = 128
RET_V_DIM = 256
CONV_WIDTH = 31
CONV_HALO = 32
RET_DECAY_OFFSET = 5.0
ROPE_BASE = 10000.0
SB_HEADS = 16
SB_HEAD_DIM = 64
D_FF = 4 * D_MODEL
EPS = 1e-6
SB_EXIT_LOG = -104.0
VMEM_LIMIT = 48 * 1024 * 1024


def _pick(m, candidates):
    for c in candidates:
        if m % c == 0:
            return c
    raise ValueError(f"no tile in {candidates} divides {m}")


def _cparams(sem):
    return pltpu.CompilerParams(dimension_semantics=sem, vmem_limit_bytes=VMEM_LIMIT)


def _rms(xf, g):
    return xf * lax.rsqrt(jnp.mean(xf * xf, axis=-1, keepdims=True) + EPS) * g


def _rmsnorm_kernel(x_ref, g_ref, o_ref):
    o_ref[...] = _rms(x_ref[...], g_ref[...]).astype(o_ref.dtype)


def rmsnorm_rows(x, g):
    m, d = x.shape
    tm = _pick(m, (1024, 512, 256, 128))
    return pl.pallas_call(
        _rmsnorm_kernel,
        grid=(m // tm,),
        in_specs=[pl.BlockSpec((tm, d), lambda i: (i, 0)),
                  pl.BlockSpec((1, d), lambda i: (0, 0))],
        out_specs=pl.BlockSpec((tm, d), lambda i: (i, 0)),
        out_shape=jax.ShapeDtypeStruct((m, d), BF16),
        compiler_params=_cparams(("parallel",)),
        name="rmsnorm",
    )(x, g.reshape(1, d))


def _matmul_kernel(a_ref, w_ref, o_ref):
    o_ref[...] = jnp.dot(a_ref[...], w_ref[...], preferred_element_type=F32).astype(o_ref.dtype)


def matmul(a, w):
    m, k = a.shape
    n = w.shape[1]
    tm = _pick(m, (1024, 512, 256, 128))
    tn = _pick(n, (1024, 512))
    return pl.pallas_call(
        _matmul_kernel,
        grid=(m // tm, n // tn),
        in_specs=[pl.BlockSpec((tm, k), lambda i, j: (i, 0)),
                  pl.BlockSpec((k, tn), lambda i, j: (0, j))],
        out_specs=pl.BlockSpec((tm, tn), lambda i, j: (i, j)),
        out_shape=jax.ShapeDtypeStruct((m, n), BF16),
        compiler_params=_cparams(("parallel", "arbitrary")),
        name="matmul",
    )(a, w)


def _head_rms(x, g):
    x2 = x * x
    lo = lax.broadcasted_iota(jnp.int32, x.shape, 1) < SB_HEAD_DIM
    s_lo = jnp.sum(jnp.where(lo, x2, 0.0), axis=-1, keepdims=True)
    s_hi = jnp.sum(jnp.where(lo, 0.0, x2), axis=-1, keepdims=True)
    ms = jnp.where(lo, s_lo, s_hi) * (1.0 / SB_HEAD_DIM)
    return x * lax.rsqrt(ms + EPS) * g


def _qkv_kernel(a_ref, w_ref, g_ref, o_ref):
    j = pl.program_id(1)
    y = jnp.dot(a_ref[...], w_ref[...], preferred_element_type=F32)

    @pl.when(j < 2)
    def _():
        g = g_ref[0]
        for c in range(y.shape[1] // 128):
            o_ref[:, c * 128:(c + 1) * 128] = _head_rms(y[:, c * 128:(c + 1) * 128], g).astype(o_ref.dtype)

    @pl.when(j == 2)
    def _():
        o_ref[...] = y.astype(o_ref.dtype)


def qkv_proj(a, w, qg, kg):
    m, k = a.shape
    n = w.shape[1]
    tm = _pick(m, (1024, 512, 256, 128))
    tn = D_MODEL
    scale = SB_HEAD_DIM ** -0.5
    g = jnp.stack([jnp.tile(qg.astype(F32) * scale, 2), jnp.tile(kg.astype(F32), 2),
                   jnp.ones((128,), F32)]).reshape(3, 1, 128)
    return pl.pallas_call(
        _qkv_kernel,
        grid=(m // tm, n // tn),
        in_specs=[pl.BlockSpec((tm, k), lambda i, j: (i, 0)),
                  pl.BlockSpec((k, tn), lambda i, j: (0, j)),
                  pl.BlockSpec((1, 1, 128), lambda i, j: (j, 0, 0))],
        out_specs=pl.BlockSpec((tm, tn), lambda i, j: (i, j)),
        out_shape=jax.ShapeDtypeStruct((m, n), BF16),
        compiler_params=_cparams(("parallel", "arbitrary")),
        name="qkv_proj",
    )(a, w, g)


def _proj_res_kernel(*refs, n_in, with_norm):
    a_refs = refs[:n_in]
    w_refs = refs[n_in:2 * n_in]
    h_ref = refs[2 * n_in]
    g_ref = refs[2 * n_in + 1]
    ho_ref = refs[2 * n_in + 2]
    acc = h_ref[...]
    for a_ref, w_ref in zip(a_refs, w_refs):
        acc = acc + jnp.dot(a_ref[...], w_ref[...], preferred_element_type=F32)
    ho_ref[...] = acc
    if with_norm:
        refs[2 * n_in + 3][...] = _rms(acc, g_ref[...]).astype(BF16)


def proj_residual(a_list, w_list, h, g_next):
    m, d = h.shape
    tm = _pick(m, (512, 256, 128))
    n_in = len(a_list)
    in_specs = [pl.BlockSpec((tm, a.shape[1]), lambda i: (i, 0)) for a in a_list]
    in_specs += [pl.BlockSpec(w.shape, lambda i: (0, 0)) for w in w_list]
    in_specs += [pl.BlockSpec((tm, d), lambda i: (i, 0)), pl.BlockSpec((1, d), lambda i: (0, 0))]
    return pl.pallas_call(
        functools.partial(_proj_res_kernel, n_in=n_in, with_norm=True),
        grid=(m // tm,),
        in_specs=in_specs,
        out_specs=[pl.BlockSpec((tm, d), lambda i: (i, 0)), pl.BlockSpec((tm, d), lambda i: (i, 0))],
        out_shape=[jax.ShapeDtypeStruct((m, d), F32), jax.ShapeDtypeStruct((m, d), BF16)],
        compiler_params=_cparams(("parallel",)),
        name="proj_residual",
    )(*a_list, *w_list, h, g_next.reshape(1, d))


def _mlp_kernel(hn_ref, w1_ref, w2_ref, h_ref, g_ref, *out_and_scratch, with_norm):
    if with_norm:
        ho_ref, hno_ref, acc_ref = out_and_scratch
    else:
        ho_ref, acc_ref = out_and_scratch
    f = pl.program_id(1)
    a = jnp.dot(hn_ref[...], w1_ref[...], preferred_element_type=F32)
    a = jnp.square(jnp.maximum(a, 0.0)).astype(BF16)
    part = jnp.dot(a, w2_ref[...], preferred_element_type=F32)

    @pl.when(f == 0)
    def _():
        acc_ref[...] = h_ref[...] + part

    @pl.when(f > 0)
    def _():
        acc_ref[...] += part

    @pl.when(f == pl.num_programs(1) - 1)
    def _():
        hnew = acc_ref[...]
        ho_ref[...] = hnew
        if with_norm:
            hno_ref[...] = _rms(hnew, g_ref[...]).astype(BF16)


def mlp_residual(hn, w1, w2, h, g_next):
    m, d = h.shape
    ff = w1.shape[1]
    tm = _pick(m, (512, 256, 128))
    tf = _pick(ff, (512,))
    with_norm = g_next is not None
    g = (g_next if with_norm else jnp.ones((d,), F32)).reshape(1, d)
    out_specs = [pl.BlockSpec((tm, d), lambda i, f: (i, 0))]
    out_shape = [jax.ShapeDtypeStruct((m, d), F32)]
    if with_norm:
        out_specs.append(pl.BlockSpec((tm, d), lambda i, f: (i, 0)))
        out_shape.append(jax.ShapeDtypeStruct((m, d), BF16))
    res = pl.pallas_call(
        functools.partial(_mlp_kernel, with_norm=with_norm),
        grid=(m // tm, ff // tf),
        in_specs=[pl.BlockSpec((tm, d), lambda i, f: (i, 0)),
                  pl.BlockSpec((d, tf), lambda i, f: (0, f)),
                  pl.BlockSpec((tf, d), lambda i, f: (f, 0)),
                  pl.BlockSpec((tm, d), lambda i, f: (i, 0)),
                  pl.BlockSpec((1, d), lambda i, f: (0, 0))],
        out_specs=out_specs,
        out_shape=out_shape,
        scratch_shapes=[pltpu.VMEM((tm, d), F32)],
        compiler_params=_cparams(("parallel", "arbitrary")),
        name="mlp",
    )(hn, w1, w2, h, g)
    return res if with_norm else (res[0], None)


def _swap_halves(x):
    half = x.shape[-1] // 2
    return jnp.concatenate([x[:, half:], x[:, :half]], axis=-1)


def _retention_kernel(q_ref, k_ref, v_ref, g_ref, cos_ref, sin_ref, dec_ref, qd_ref, kd_ref,
                      gn_ref, o_ref, s_ref):
    n_chunks = q_ref.shape[1] // CHUNK
    s_ref[...] = jnp.zeros_like(s_ref)
    dec = dec_ref[0]
    qd = qd_ref[0]
    kd = kd_ref[0]
    chunk_decay = qd[CHUNK - 1:CHUNK, 0:1]
    gn = gn_ref[0]

    def body(n, carry):
        r = pl.multiple_of(n * CHUNK, CHUNK)
        rows = pl.ds(r, CHUNK)
        q = q_ref[0, rows, :].astype(F32)
        k = k_ref[0, rows, :].astype(F32)
        v = v_ref[0, rows, :]
        cos = cos_ref[rows, :]
        sin = sin_ref[rows, :]
        qr = q * cos + _swap_halves(q) * sin
        kr = (k * cos + _swap_halves(k) * sin) * (RET_QK_DIM ** -0.5)
        scores = lax.dot_general(qr.astype(BF16), kr.astype(BF16), (((1,), (1,)), ((), ())),
                                 preferred_element_type=F32) * dec
        o = jnp.dot(scores.astype(BF16), v, preferred_element_type=F32)
        state = s_ref[...]
        o = o + jnp.dot((qr * qd).astype(BF16), state.astype(BF16), preferred_element_type=F32)
        kv = lax.dot_general((kr * kd).astype(BF16), v, (((0,), (0,)), ((), ())),
                             preferred_element_type=F32)
        s_ref[...] = chunk_decay * state + kv
        mu = jnp.mean(o, axis=-1, keepdims=True)
        oc = o - mu
        var = jnp.mean(oc * oc, axis=-1, keepdims=True)
        on = oc * lax.rsqrt(var + EPS) * gn
        gate = g_ref[0, rows, :].astype(F32)
        o_ref[0, rows, :] = (gate * jax.nn.sigmoid(gate) * on).astype(o_ref.dtype)
        return carry

    lax.fori_loop(0, n_chunks, body, 0)


def _retention_tables(p):
    half = RET_QK_DIM // 2
    inv_freq = ROPE_BASE ** (-np.arange(half, dtype=np.float64) / half)
    ang = np.arange(p, dtype=np.float64)[:, None] * inv_freq[None, :]
    cos, sin = np.cos(ang), np.sin(ang)
    cos_t = np.concatenate([cos, cos], axis=1)
    sin_t = np.concatenate([-sin, sin], axis=1)
    log_g = np.log1p(-np.exp2(-RET_DECAY_OFFSET - np.arange(RET_HEADS, dtype=np.float64)))
    idx = np.arange(CHUNK, dtype=np.float64)
    diff = idx[:, None] - idx[None, :]
    dec = np.where(diff[None] >= 0, np.exp(np.maximum(diff, 0.0)[None] * log_g[:, None, None]), 0.0)
    qd = np.exp((idx + 1.0)[None, :] * log_g[:, None])
    kd = np.exp((CHUNK - 1 - idx)[None, :] * log_g[:, None])
    bc = lambda t: np.broadcast_to(t[:, :, None], (RET_HEADS, CHUNK, RET_QK_DIM))
    f32 = lambda t: jnp.asarray(np.asarray(t, dtype=np.float32))
    return f32(cos_t), f32(sin_t), f32(dec), f32(bc(qd)), f32(bc(kd))


def retention(proj, gn_g):
    b, p, _ = proj.shape
    cos_t, sin_t, dec, qd, kd = _retention_tables(p)
    nq = RET_HEADS
    nv = (2 * RET_HEADS * RET_QK_DIM) // RET_V_DIM
    return pl.pallas_call(
        _retention_kernel,
        grid=(b, RET_HEADS),
        in_specs=[pl.BlockSpec((1, p, RET_QK_DIM), lambda i, h: (i, 0, h)),
                  pl.BlockSpec((1, p, RET_QK_DIM), lambda i, h: (i, 0, nq + h)),
                  pl.BlockSpec((1, p, RET_V_DIM), lambda i, h: (i, 0, nv + h)),
                  pl.BlockSpec((1, p, RET_V_DIM), lambda i, h: (i, 0, nv + RET_HEADS + h)),
                  pl.BlockSpec((p, RET_QK_DIM), lambda i, h: (0, 0)),
                  pl.BlockSpec((p, RET_QK_DIM), lambda i, h: (0, 0)),
                  pl.BlockSpec((1, CHUNK, CHUNK), lambda i, h: (h, 0, 0)),
                  pl.BlockSpec((1, CHUNK, RET_QK_DIM), lambda i, h: (h, 0, 0)),
                  pl.BlockSpec((1, CHUNK, RET_QK_DIM), lambda i, h: (h, 0, 0)),
                  pl.BlockSpec((1, 1, RET_V_DIM), lambda i, h: (h, 0, 0))],
        out_specs=pl.BlockSpec((1, p, RET_V_DIM), lambda i, h: (i, 0, h)),
        out_shape=jax.ShapeDtypeStruct((b, p, RET_HEADS * RET_V_DIM), BF16),
        scratch_shapes=[pltpu.VMEM((RET_QK_DIM, RET_V_DIM), F32)],
        compiler_params=_cparams(("parallel", "parallel")),
        name="retention",
    )(proj, proj, proj, proj, cos_t, sin_t, dec, qd, kd,
      gn_g.astype(F32).reshape(RET_HEADS, 1, RET_V_DIM))


CONV_ROWS = 64


def _conv_kernel(a_ref, gate_ref, w_ref, b_ref, lg_ref, lb_ref, o_ref, buf_ref, y_ref):
    t = pl.program_id(1)
    tt = a_ref.shape[1]

    @pl.when(t == 0)
    def _():
        buf_ref[0:CONV_HALO, :] = jnp.zeros((CONV_HALO, buf_ref.shape[1]), F32)

    @pl.when(t > 0)
    def _():
        buf_ref[0:CONV_HALO, :] = buf_ref[tt:tt + CONV_HALO, :]

    a = a_ref[0].astype(F32)
    gate = gate_ref[0].astype(F32)
    buf_ref[CONV_HALO:CONV_HALO + tt, :] = a * jax.nn.sigmoid(gate)

    first = CONV_HALO - (CONV_WIDTH - 1)
    for c in range(a_ref.shape[2] // 128):
        cols = slice(c * 128, (c + 1) * 128)
        for rb in range(tt // CONV_ROWS):
            acc = jnp.broadcast_to(b_ref[:, cols], (CONV_ROWS, 128))
            for j in range(CONV_WIDTH):
                r0 = first + j + rb * CONV_ROWS
                acc = acc + w_ref[j:j + 1, cols] * buf_ref[r0:r0 + CONV_ROWS, cols]
            y_ref[rb * CONV_ROWS:(rb + 1) * CONV_ROWS, cols] = acc

    y = y_ref[...]
    mu = jnp.mean(y, axis=-1, keepdims=True)
    yc = y - mu
    var = jnp.mean(yc * yc, axis=-1, keepdims=True)
    z = yc * lax.rsqrt(var + EPS) * lg_ref[...] + lb_ref[...]
    o_ref[0] = (z * jax.nn.sigmoid(z)).astype(o_ref.dtype)


def conformer_conv(proj, conv_w, conv_b, ln_g, ln_b):
    b, p, n = proj.shape
    ch = D_MODEL
    tt = CHUNK
    a_blk = (n - 2 * ch) // ch
    row = lambda x: x.astype(F32).reshape(1, ch)
    return pl.pallas_call(
        _conv_kernel,
        grid=(b, p // tt),
        in_specs=[pl.BlockSpec((1, tt, ch), lambda i, t: (i, t, a_blk)),
                  pl.BlockSpec((1, tt, ch), lambda i, t: (i, t, a_blk + 1)),
                  pl.BlockSpec((CONV_WIDTH, ch), lambda i, t: (0, 0)),
                  pl.BlockSpec((1, ch), lambda i, t: (0, 0)),
                  pl.BlockSpec((1, ch), lambda i, t: (0, 0)),
                  pl.BlockSpec((1, ch), lambda i, t: (0, 0))],
        out_specs=pl.BlockSpec((1, tt, ch), lambda i, t: (i, t, 0)),
        out_shape=jax.ShapeDtypeStruct((b, p, ch), BF16),
        scratch_shapes=[pltpu.VMEM((CONV_HALO + tt, ch), F32), pltpu.VMEM((tt, ch), F32)],
        compiler_params=_cparams(("parallel", "arbitrary")),
        name="conformer_conv",
    )(proj, proj, conv_w.astype(F32), row(conv_b), row(ln_g), row(ln_b))


def _sb_kernel(q_ref, k_ref, v_ref, u_ref, o_ref, acc_ref, carry_ref):
    n_blocks = q_ref.shape[1] // CHUNK
    lane = lax.broadcasted_iota(jnp.int32, (CHUNK, CHUNK), 1)
    rowi = lax.broadcasted_iota(jnp.int32, (CHUNK, CHUNK), 0)
    head0 = lane < SB_HEAD_DIM
    u = u_ref[...]

    def one_block(qs, j, qi):
        kr = pl.ds(pl.multiple_of(j * CHUNK, CHUNK), CHUNK)
        kb = k_ref[0, kr, :]
        vb = v_ref[0, kr, :]
        key_pos = j * CHUNK + lane
        valid = (key_pos < qi * CHUNK + rowi) & (key_pos >= PAD_FRONT)
        for a in range(2):
            z = lax.dot_general(qs[a], kb, (((1,), (1,)), ((), ())), preferred_element_type=F32)
            lk = -(jnp.maximum(z, 0.0) + jnp.log(1.0 + jnp.exp(-jnp.abs(z))))
            lk = jnp.where(valid, lk, 0.0)
            hi = lk.astype(BF16)
            lo = (lk - hi.astype(F32)).astype(BF16)
            sums = jnp.dot(jnp.concatenate([hi, lo], axis=1), u, preferred_element_type=F32)
            incl = sums[:, :CHUNK]
            total = sums[:, CHUNK:]
            carry = carry_ref[a]
            w = jnp.where(valid, jnp.exp(z + incl + carry), 0.0)
            acc_ref[a] += jnp.dot(w.astype(BF16), vb, preferred_element_type=F32)
            carry_ref[a] = carry + total

    def q_block(qi, c):
        qr = pl.ds(pl.multiple_of(qi * CHUNK, CHUNK), CHUNK)
        q = q_ref[0, qr, :]
        zero = jnp.zeros_like(q)
        qs = (jnp.where(head0, q, zero), jnp.where(head0, zero, q))
        acc_ref[...] = jnp.zeros_like(acc_ref)
        carry_ref[...] = jnp.zeros_like(carry_ref)

        def cond(j):
            live = jnp.max(jnp.maximum(carry_ref[0], carry_ref[1])) >= SB_EXIT_LOG
            return jnp.logical_and(j >= 0, live)

        def body(j):
            one_block(qs, j, qi)
            return j - 1

        lax.while_loop(cond, body, qi)
        o_ref[0, qr, :] = jnp.where(head0, acc_ref[0], acc_ref[1]).astype(o_ref.dtype)
        return c

    lax.fori_loop(0, n_blocks, q_block, 0)


def _sb_sum_matrix():
    s = np.arange(CHUNK)
    incl = (s[:, None] >= s[None, :]).astype(np.float32)
    half = np.concatenate([incl, np.ones((CHUNK, CHUNK), np.float32)], axis=1)
    return jnp.asarray(np.concatenate([half, half], axis=0), dtype=BF16)


def stick_breaking(qkv):
    b, p, _ = qkv.shape
    npair = D_MODEL // 128
    return pl.pallas_call(
        _sb_kernel,
        grid=(b, npair),
        in_specs=[pl.BlockSpec((1, p, 128), lambda i, h: (i, 0, h)),
                  pl.BlockSpec((1, p, 128), lambda i, h: (i, 0, npair + h)),
                  pl.BlockSpec((1, p, 128), lambda i, h: (i, 0, 2 * npair + h)),
                  pl.BlockSpec((2 * CHUNK, 2 * CHUNK), lambda i, h: (0, 0))],
        out_specs=pl.BlockSpec((1, p, 128), lambda i, h: (i, 0, h)),
        out_shape=jax.ShapeDtypeStruct((b, p, D_MODEL), BF16),
        scratch_shapes=[pltpu.VMEM((2, CHUNK, CHUNK), F32), pltpu.VMEM((2, CHUNK, CHUNK), F32)],
        compiler_params=_cparams(("parallel", "parallel")),
        name="stick_breaking",
    )(qkv, qkv, qkv, _sb_sum_matrix())


def kernel(x, meta, norm_mix_g, norm_mlp_g, even_w_in, even_ret_gn_g, even_conv_w, even_conv_b,
           even_conv_ln_g, even_conv_ln_b, even_w_out, odd_w_qkv, odd_q_norm_g, odd_k_norm_g,
           odd_w_o, mlp_w1, mlp_w2):
    b, seq, d = x.shape
    depth = norm_mix_g.shape[0]
    p = PAD_FRONT + N_META + seq
    assert d == D_MODEL and p % CHUNK == 0
    m = b * p
    head = jnp.concatenate([jnp.zeros((PAD_FRONT, d), x.dtype), meta.astype(x.dtype)], axis=0)
    h = jnp.concatenate([jnp.broadcast_to(head[None], (b, PAD_FRONT + N_META, d)), x], axis=1)
    h = h.reshape(m, d)
    hn = rmsnorm_rows(h, norm_mix_g[0])
    for layer in range(depth):
        j = layer // 2
        if layer % 2 == 0:
            proj = matmul(hn, even_w_in[j].astype(BF16)).reshape(b, p, -1)
            o = retention(proj, even_ret_gn_g[j])
            c = conformer_conv(proj, even_conv_w[j], even_conv_b[j], even_conv_ln_g[j], even_conv_ln_b[j])
            w_out = even_w_out[j].astype(BF16)
            n_o = o.shape[-1]
            h, hn = proj_residual([o.reshape(m, -1), c.reshape(m, -1)], [w_out[:n_o], w_out[n_o:]],
                                  h, norm_mlp_g[layer])
        else:
            qkv = qkv_proj(hn, odd_w_qkv[j].astype(BF16), odd_q_norm_g[j], odd_k_norm_g[j])
            o = stick_breaking(qkv.reshape(b, p, -1))
            h, hn = proj_residual([o.reshape(m, -1)], [odd_w_o[j].astype(BF16)], h, norm_mlp_g[layer])
        g_next = norm_mix_g[layer + 1] if layer + 1 < depth else None
        h, hn = mlp_residual(hn, mlp_w1[layer].astype(BF16), mlp_w2[layer].astype(BF16), h, g_next)
    return h.reshape(b, p, d)[:, PAD_FRONT + N_META:]
```

```python
import functools

import numpy as np
import jax
import jax.numpy as jnp
from jax import lax
from jax.experimental import pallas as pl
from jax.experimental.pallas import tpu as pltpu

F32 = jnp.float32
BF16 = jnp.bfloat16

D_MODEL = 1024
N_META = 16
CHUNK = 128
PAD_FRONT = (-N_META) % CHUNK
RET_HEADS = 4
RET_QK_DIM = 128
RET_V_DIM = 256
CONV_WIDTH = 31
CONV_HALO = 32
SUBLANES = 8
RET_DECAY_OFFSET = 5.0
ROPE_BASE = 10000.0
SB_HEADS = 16
SB_HEAD_DIM = 64
D_FF = 4 * D_MODEL
EPS = 1e-6
SB_EXIT_LOG = -104.0
VMEM_LIMIT = 48 * 1024 * 1024


def _pick(m, candidates):
    for c in candidates:
        if m % c == 0:
            return c
    raise ValueError(f"no tile in {candidates} divides {m}")


def _cparams(sem):
    return pltpu.CompilerParams(dimension_semantics=sem, vmem_limit_bytes=VMEM_LIMIT)


def _resident(shape):
    return pl.BlockSpec(shape, lambda *_: (0,) * len(shape), pipeline_mode=pl.Buffered(1))


def _rms(xf, g):
    return xf * lax.rsqrt(jnp.mean(xf * xf, axis=-1, keepdims=True) + EPS) * g


def _norm_matmul_kernel(x_ref, g_ref, w_ref, o_ref, *, n_chunks):
    hn = _rms(x_ref[...], g_ref[...]).astype(BF16)
    tn = w_ref.shape[1] // n_chunks
    for c in range(n_chunks):
        cols = slice(c * tn, (c + 1) * tn)
        o_ref[:, cols] = jnp.dot(hn, w_ref[:, cols], preferred_element_type=F32).astype(o_ref.dtype)


def norm_matmul(x, g, w):
    m, d = x.shape
    n = w.shape[1]
    tm = _pick(m, (512, 256, 128))
    return pl.pallas_call(
        functools.partial(_norm_matmul_kernel, n_chunks=n // _pick(n, (1024, 512))),
        grid=(m // tm,),
        in_specs=[pl.BlockSpec((tm, d), lambda i: (i, 0)), _resident((1, d)), _resident(w.shape)],
        out_specs=pl.BlockSpec((tm, n), lambda i: (i, 0)),
        out_shape=jax.ShapeDtypeStruct((m, n), BF16),
        compiler_params=_cparams(("parallel",)),
        name="norm_matmul",
    )(x, g.reshape(1, d), w)


def _head_rms(x, g):
    x2 = x * x
    lo = lax.broadcasted_iota(jnp.int32, x.shape, 1) < SB_HEAD_DIM
    s_lo = jnp.sum(jnp.where(lo, x2, 0.0), axis=-1, keepdims=True)
    s_hi = jnp.sum(jnp.where(lo, 0.0, x2), axis=-1, keepdims=True)
    ms = jnp.where(lo, s_lo, s_hi) * (1.0 / SB_HEAD_DIM)
    return x * lax.rsqrt(ms + EPS) * g


def _qkv_kernel(a_ref, w_ref, g_ref, o_ref):
    a = a_ref[...]
    d = a.shape[1]
    for part in range(3):
        y = jnp.dot(a, w_ref[:, part * d:(part + 1) * d], preferred_element_type=F32)
        if part == 2:
            o_ref[:, part * d:(part + 1) * d] = y.astype(o_ref.dtype)
        else:
            g = g_ref[part:part + 1, :]
            for c in range(d // 128):
                o_ref[:, part * d + c * 128:part * d + (c + 1) * 128] = (
                    _head_rms(y[:, c * 128:(c + 1) * 128], g).astype(o_ref.dtype))


def qkv_proj(a, w, qg, kg):
    m, k = a.shape
    n = w.shape[1]
    tm = _pick(m, (512, 256, 128))
    scale = SB_HEAD_DIM ** -0.5
    g = jnp.stack([jnp.tile(qg.astype(F32) * scale, 2), jnp.tile(kg.astype(F32), 2)])
    return pl.pallas_call(
        _qkv_kernel,
        grid=(m // tm,),
        in_specs=[pl.BlockSpec((tm, k), lambda i: (i, 0)), _resident(w.shape), _resident(g.shape)],
        out_specs=pl.BlockSpec((tm, n), lambda i: (i, 0)),
        out_shape=jax.ShapeDtypeStruct((m, n), BF16),
        compiler_params=_cparams(("parallel",)),
        name="qkv_proj",
    )(a, w, g)


def _proj_res_kernel(*refs, n_in):
    a_refs = refs[:n_in]
    w_refs = refs[n_in:2 * n_in]
    h_ref, g_ref, ho_ref, hno_ref = refs[2 * n_in:]
    acc = h_ref[...]
    for a_ref, w_ref in zip(a_refs, w_refs):
        acc = acc + jnp.dot(a_ref[...], w_ref[...], preferred_element_type=F32)
    ho_ref[...] = acc
    hno_ref[...] = _rms(acc, g_ref[...]).astype(BF16)


def proj_residual(a_list, w_list, h, g_next):
    m, d = h.shape
    tm = _pick(m, (512, 256, 128))
    in_specs = [pl.BlockSpec((tm, a.shape[1]), lambda i: (i, 0)) for a in a_list]
    in_specs += [_resident(w.shape) for w in w_list]
    in_specs += [pl.BlockSpec((tm, d), lambda i: (i, 0)), _resident((1, d))]
    return pl.pallas_call(
        functools.partial(_proj_res_kernel, n_in=len(a_list)),
        grid=(m // tm,),
        in_specs=in_specs,
        out_specs=[pl.BlockSpec((tm, d), lambda i: (i, 0)), pl.BlockSpec((tm, d), lambda i: (i, 0))],
        out_shape=[jax.ShapeDtypeStruct((m, d), F32), jax.ShapeDtypeStruct((m, d), BF16)],
        compiler_params=_cparams(("parallel",)),
        name="proj_residual",
    )(*a_list, *w_list, h, g_next.reshape(1, d))


def _mlp_kernel(hn_ref, w1_ref, w2_ref, h_ref, g_ref, *out_refs, n_chunks, with_norm):
    hn = hn_ref[...]
    tf = w1_ref.shape[1] // n_chunks
    acc = h_ref[...]
    for c in range(n_chunks):
        a = jnp.dot(hn, w1_ref[:, c * tf:(c + 1) * tf], preferred_element_type=F32)
        a = jnp.square(jnp.maximum(a, 0.0)).astype(BF16)
        acc = acc + jnp.dot(a, w2_ref[c * tf:(c + 1) * tf, :], preferred_element_type=F32)
    out_refs[0][...] = acc
    if with_norm:
        out_refs[1][...] = _rms(acc, g_ref[...]).astype(BF16)


def mlp_residual(hn, w1, w2, h, g_next):
    m, d = h.shape
    ff = w1.shape[1]
    tm = _pick(m, (512, 256, 128))
    with_norm = g_next is not None
    g = (g_next if with_norm else jnp.ones((d,), F32)).reshape(1, d)
    row_spec = pl.BlockSpec((tm, d), lambda i: (i, 0))
    out_specs = [row_spec]
    out_shape = [jax.ShapeDtypeStruct((m, d), F32)]
    if with_norm:
        out_specs.append(row_spec)
        out_shape.append(jax.ShapeDtypeStruct((m, d), BF16))
    res = pl.pallas_call(
        functools.partial(_mlp_kernel, n_chunks=ff // _pick(ff, (512,)), with_norm=with_norm),
        grid=(m // tm,),
        in_specs=[row_spec, _resident(w1.shape), _resident(w2.shape), row_spec, _resident((1, d))],
        out_specs=out_specs,
        out_shape=out_shape,
        compiler_params=_cparams(("parallel",)),
        name="mlp",
    )(hn, w1, w2, h, g)
    return res if with_norm else (res[0], None)


def _swap_halves(x):
    half = x.shape[-1] // 2
    return jnp.concatenate([x[:, half:], x[:, :half]], axis=-1)


def _retention_kernel(q_ref, k_ref, v_ref, g_ref, cos_ref, sin_ref, dec_ref, qd_ref, kd_ref,
                      gn_ref, o_ref, s_ref):
    @pl.when(pl.program_id(1) == 0)
    def _():
        s_ref[...] = jnp.zeros_like(s_ref)

    def body(n, carry):
        rows = pl.ds(pl.multiple_of(n * CHUNK, CHUNK), CHUNK)
        cos = cos_ref[rows, :]
        sin = sin_ref[rows, :]
        for h in range(RET_HEADS):
            qk_cols = slice(h * RET_QK_DIM, (h + 1) * RET_QK_DIM)
            v_cols = slice(h * RET_V_DIM, (h + 1) * RET_V_DIM)
            qd = qd_ref[h]
            q = q_ref[0, rows, qk_cols].astype(F32)
            k = k_ref[0, rows, qk_cols].astype(F32)
            v = v_ref[0, rows, v_cols]
            qr = q * cos + _swap_halves(q) * sin
            kr = (k * cos + _swap_halves(k) * sin) * (RET_QK_DIM ** -0.5)
            scores = lax.dot_general(qr.astype(BF16), kr.astype(BF16), (((1,), (1,)), ((), ())),
                                     preferred_element_type=F32) * dec_ref[h]
            o = jnp.dot(scores.astype(BF16), v, preferred_element_type=F32)
            state = s_ref[h]
            o = o + jnp.dot((qr * qd).astype(BF16), state.astype(BF16), preferred_element_type=F32)
            kv = lax.dot_general((kr * kd_ref[h]).astype(BF16), v, (((0,), (0,)), ((), ())),
                                 preferred_element_type=F32)
            s_ref[h] = qd[CHUNK - 1:CHUNK, 0:1] * state + kv
            mu = jnp.mean(o, axis=-1, keepdims=True)
            oc = o - mu
            var = jnp.mean(oc * oc, axis=-1, keepdims=True)
            on = oc * lax.rsqrt(var + EPS) * gn_ref[h]
            gate = g_ref[0, rows, v_cols].astype(F32)
            o_ref[0, rows, v_cols] = (gate * jax.nn.sigmoid(gate) * on).astype(o_ref.dtype)
        return carry

    lax.fori_loop(0, q_ref.shape[1] // CHUNK, body, 0)


def _retention_tables(p):
    half = RET_QK_DIM // 2
    inv_freq = ROPE_BASE ** (-np.arange(half, dtype=np.float64) / half)
    ang = np.arange(p, dtype=np.float64)[:, None] * inv_freq[None, :]
    cos, sin = np.cos(ang), np.sin(ang)
    cos_t = np.concatenate([cos, cos], axis=1)
    sin_t = np.concatenate([-sin, sin], axis=1)
    log_g = np.log1p(-np.exp2(-RET_DECAY_OFFSET - np.arange(RET_HEADS, dtype=np.float64)))
    idx = np.arange(CHUNK, dtype=np.float64)
    diff = idx[:, None] - idx[None, :]
    dec = np.where(diff[None] >= 0, np.exp(np.maximum(diff, 0.0)[None] * log_g[:, None, None]), 0.0)
    qd = np.exp((idx + 1.0)[None, :] * log_g[:, None])
    kd = np.exp((CHUNK - 1 - idx)[None, :] * log_g[:, None])
    bc = lambda t: np.broadcast_to(t[:, :, None], (RET_HEADS, CHUNK, RET_QK_DIM))
    f32 = lambda t: jnp.asarray(np.asarray(t, dtype=np.float32))
    return f32(cos_t), f32(sin_t), f32(dec), f32(bc(qd)), f32(bc(kd))


def retention(proj, gn_g):
    b, p, _ = proj.shape
    cos_t, sin_t, dec, qd, kd = _retention_tables(p)
    qk_w = RET_HEADS * RET_QK_DIM
    v_w = RET_HEADS * RET_V_DIM
    tt = CHUNK * _pick(p // CHUNK, (11, 3, 1))
    return pl.pallas_call(
        _retention_kernel,
        grid=(b, p // tt),
        in_specs=[pl.BlockSpec((1, tt, qk_w), lambda i, t: (i, t, 0)),
                  pl.BlockSpec((1, tt, qk_w), lambda i, t: (i, t, 1)),
                  pl.BlockSpec((1, tt, v_w), lambda i, t: (i, t, (2 * qk_w) // v_w)),
                  pl.BlockSpec((1, tt, v_w), lambda i, t: (i, t, (2 * qk_w) // v_w + 1)),
                  pl.BlockSpec((tt, RET_QK_DIM), lambda i, t: (t, 0)),
                  pl.BlockSpec((tt, RET_QK_DIM), lambda i, t: (t, 0)),
                  _resident(dec.shape), _resident(qd.shape), _resident(kd.shape),
                  _resident((RET_HEADS, 1, RET_V_DIM))],
        out_specs=pl.BlockSpec((1, tt, v_w), lambda i, t: (i, t, 0)),
        out_shape=jax.ShapeDtypeStruct((b, p, v_w), BF16),
        scratch_shapes=[pltpu.VMEM((RET_HEADS, RET_QK_DIM, RET_V_DIM), F32)],
        compiler_params=_cparams(("parallel", "arbitrary")),
        name="retention",
    )(proj, proj, proj, proj, cos_t, sin_t, dec, qd, kd,
      gn_g.astype(F32).reshape(RET_HEADS, 1, RET_V_DIM))


CONV_ROWS = 64


def _conv_kernel(a_ref, gate_ref, w_ref, b_ref, lg_ref, lb_ref, o_ref, sh_ref, y_ref):
    t = pl.program_id(1)
    tt = a_ref.shape[1]
    ch = a_ref.shape[2]
    n_sh = tt + CONV_HALO - SUBLANES

    @pl.when(t == 0)
    def _():
        sh_ref[0, 0:CONV_HALO, :] = jnp.zeros((CONV_HALO, ch), F32)

    @pl.when(t > 0)
    def _():
        sh_ref[0, 0:CONV_HALO, :] = sh_ref[0, tt:tt + CONV_HALO, :]

    gate = gate_ref[0].astype(F32)
    sh_ref[0, CONV_HALO:CONV_HALO + tt, :] = a_ref[0].astype(F32) * jax.nn.sigmoid(gate)
    for r in range(1, SUBLANES):
        sh_ref[r, 0:n_sh, :] = sh_ref[0, r:r + n_sh, :]

    first = CONV_HALO - (CONV_WIDTH - 1)

    def row_block(rb, carry):
        base = pl.multiple_of(rb * CONV_ROWS, CONV_ROWS)
        for c in range(ch // 128):
            cols = slice(c * 128, (c + 1) * 128)
            accs = [jnp.broadcast_to(b_ref[:, cols], (SUBLANES, 128))] * (CONV_ROWS // SUBLANES)
            for j in range(CONV_WIDTH):
                blk, r = divmod(first + j, SUBLANES)
                wj = jnp.broadcast_to(w_ref[j:j + 1, cols], (SUBLANES, 128))
                for k in range(len(accs)):
                    x = sh_ref[r, pl.ds(base + (blk + k) * SUBLANES, SUBLANES), cols]
                    accs[k] = accs[k] + wj * x
            for k, acc in enumerate(accs):
                y_ref[k * SUBLANES:(k + 1) * SUBLANES, cols] = acc
        y = y_ref[...]
        mu = jnp.mean(y, axis=-1, keepdims=True)
        yc = y - mu
        var = jnp.mean(yc * yc, axis=-1, keepdims=True)
        z = yc * lax.rsqrt(var + EPS) * lg_ref[...] + lb_ref[...]
        o_ref[0, pl.ds(base, CONV_ROWS), :] = (z * jax.nn.sigmoid(z)).astype(o_ref.dtype)
        return carry

    lax.fori_loop(0, tt // CONV_ROWS, row_block, 0)


def conformer_conv(proj, conv_w, conv_b, ln_g, ln_b):
    b, p, n = proj.shape
    ch = D_MODEL
    tt = _pick(p, (384, 128))
    a_blk = (n - 2 * ch) // ch
    row = lambda x: x.astype(F32).reshape(1, ch)
    return pl.pallas_call(
        _conv_kernel,
        grid=(b, p // tt),
        in_specs=[pl.BlockSpec((1, tt, ch), lambda i, t: (i, t, a_blk)),
                  pl.BlockSpec((1, tt, ch), lambda i, t: (i, t, a_blk + 1)),
                  _resident((CONV_WIDTH, ch)), _resident((1, ch)), _resident((1, ch)), _resident((1, ch))],
        out_specs=pl.BlockSpec((1, tt, ch), lambda i, t: (i, t, 0)),
        out_shape=jax.ShapeDtypeStruct((b, p, ch), BF16),
        scratch_shapes=[pltpu.VMEM((SUBLANES, CONV_HALO + tt, ch), F32),
                        pltpu.VMEM((CONV_ROWS, ch), F32)],
        compiler_params=_cparams(("parallel", "arbitrary")),
        name="conformer_conv",
    )(proj, proj, conv_w.astype(F32), row(conv_b), row(ln_g), row(ln_b))


SB_WINDOW = 3
SB_GROUP = 3
LOG2E = 1.4426950408889634


def _softplus(z):
    return jnp.maximum(z, 0.0) + jnp.log(1.0 + jnp.exp2(jnp.abs(z) * -LOG2E))


def _split_bf16(x):
    hi = x.astype(BF16)
    return hi, (x - hi.astype(F32)).astype(BF16)


def _sb_kernel(q_ref, k_ref, v_ref, u_ref, o_ref, kt_ref, hl_ref, w_ref, acc_ref, csum_ref):
    n_blocks = q_ref.shape[1] // CHUNK
    n_groups = n_blocks // SB_GROUP
    lane = lax.broadcasted_iota(jnp.int32, (CHUNK, CHUNK), 1)
    rowi = lax.broadcasted_iota(jnp.int32, (CHUNK, CHUNK), 0)
    head0 = lane < SB_HEAD_DIM
    tri = lane < rowi
    lane2 = lax.broadcasted_iota(jnp.int32, (2 * CHUNK, CHUNK), 1)
    row2 = lax.broadcasted_iota(jnp.int32, (2 * CHUNK, CHUNK), 0) & (CHUNK - 1)

    def rows(j):
        if isinstance(j, int):
            return slice(j * CHUNK, (j + 1) * CHUNK)
        return pl.ds(pl.multiple_of(j * CHUNK, CHUNK), CHUNK)

    for j in range(n_blocks):
        kt_ref[j] = k_ref[0, rows(j), :].T

    def stacked_q(i):
        q = q_ref[0, rows(i), :]
        zero = jnp.zeros_like(q)
        return jnp.concatenate([jnp.where(head0, q, zero), jnp.where(head0, zero, q)], axis=0)

    def tile(x, a, blk):
        return x[a * CHUNK:(a + 1) * CHUNK, blk * CHUNK:(blk + 1) * CHUNK]

    def slot_rows(u, a, blk):
        r0 = ((u * SB_WINDOW + blk) * 2 + a) * CHUNK
        return slice(r0, r0 + CHUNK)

    def masked_block(u, qs, i, j):
        key_pos = j * CHUNK + lane2
        valid = (key_pos < i * CHUNK + row2) & (key_pos >= PAD_FRONT)
        z = jnp.dot(qs, kt_ref[j], preferred_element_type=F32)
        sp = jnp.where(valid, _softplus(z), 0.0)
        hi, lo = _split_bf16(sp)
        sums = jnp.dot(jnp.concatenate([hi, lo], axis=1), u_ref[...], preferred_element_type=F32)
        csum = csum_ref[u]
        w = jnp.where(valid, jnp.exp(z - sums[:, :CHUNK] - csum), 0.0)
        acc_ref[u] += jnp.dot(w.astype(BF16), v_ref[0, rows(j), :], preferred_element_type=F32)
        csum_ref[u] = csum + sums[:, CHUNK:]

    def window_scores(u, qs, i, masked):
        js = [i - (SB_WINDOW - 1) + blk for blk in range(SB_WINDOW)]
        jc = [max(j, 0) for j in js] if masked else js
        kt = jnp.concatenate([kt_ref[j] for j in jc], axis=1)
        zs = jnp.dot(qs, kt, preferred_element_type=F32)
        valids = []
        for blk in range(SB_WINDOW):
            if masked:
                key_pos = js[blk] * CHUNK + lane
                valid = (key_pos < i * CHUNK + rowi) & (key_pos >= PAD_FRONT)
            else:
                valid = tri if blk == SB_WINDOW - 1 else None
            valids.append(valid)
            for a in range(2):
                sp = _softplus(tile(zs, a, blk))
                if valid is not None:
                    sp = jnp.where(valid, sp, 0.0)
                hi, lo = _split_bf16(sp)
                hl_ref[slot_rows(u, a, blk), 0:CHUNK] = hi
                hl_ref[slot_rows(u, a, blk), CHUNK:2 * CHUNK] = lo
        return zs, valids, jc

    def window_weights(u, zs, valids, jc, sums):
        for a in range(2):
            csum = None
            for blk in reversed(range(SB_WINDOW)):
                x = tile(zs, a, blk) - sums[slot_rows(u, a, blk), :CHUNK]
                if csum is not None:
                    x = x - csum
                w = jnp.exp(x)
                if valids[blk] is not None:
                    w = jnp.where(valids[blk], w, 0.0)
                w_ref[u, a * CHUNK:(a + 1) * CHUNK, blk * CHUNK:(blk + 1) * CHUNK] = w.astype(BF16)
                tot = sums[slot_rows(u, a, blk), CHUNK:]
                csum = tot if csum is None else csum + tot
            csum_ref[u, a * CHUNK:(a + 1) * CHUNK, :] = csum
        v_cat = jnp.concatenate([v_ref[0, rows(j), :] for j in jc], axis=0)
        acc_ref[u] = jnp.dot(w_ref[u], v_cat, preferred_element_type=F32)

    def finish(u, qs, i, j_start):
        def cond(j):
            return jnp.logical_and(j >= 0, jnp.min(csum_ref[u]) <= -SB_EXIT_LOG)

        def body(j):
            masked_block(u, qs, i, j)
            return j - 1

        lax.while_loop(cond, body, j_start)

    def write_out(u, i):
        acc = acc_ref[u]
        o_ref[0, rows(i), :] = jnp.where(head0, acc[:CHUNK], acc[CHUNK:]).astype(o_ref.dtype)

    def group(i0, masked):
        qss = [stacked_q(i0 + u) for u in range(SB_GROUP)]
        scored = [window_scores(u, qss[u], i0 + u, masked) for u in range(SB_GROUP)]
        sums = jnp.dot(hl_ref[...], u_ref[...], preferred_element_type=F32)
        for u in range(SB_GROUP):
            window_weights(u, *scored[u], sums)
        low = csum_ref[0]
        for u in range(1, SB_GROUP):
            low = jnp.minimum(low, csum_ref[u])

        @pl.when(jnp.min(low) <= -SB_EXIT_LOG)
        def _():
            for u in range(SB_GROUP):
                finish(u, qss[u], i0 + u, i0 + u - SB_WINDOW)

        for u in range(SB_GROUP):
            write_out(u, i0 + u)

    def group_body(g, c):
        group(g * SB_GROUP, False)
        return c

    def tail_block(i, c):
        acc_ref[0] = jnp.zeros(acc_ref.shape[1:], F32)
        csum_ref[0] = jnp.zeros(csum_ref.shape[1:], F32)
        finish(0, stacked_q(i), i, i)
        write_out(0, i)
        return c

    if n_groups > 0:
        group(0, True)
        lax.fori_loop(1, n_groups, group_body, 0)
    lax.fori_loop(n_groups * SB_GROUP, n_blocks, tail_block, 0)


def _sb_sum_matrix():
    s = np.arange(CHUNK)
    incl = (s[:, None] >= s[None, :]).astype(np.float32)
    half = np.concatenate([incl, np.ones((CHUNK, CHUNK), np.float32)], axis=1)
    return jnp.asarray(np.concatenate([half, half], axis=0), dtype=BF16)


def stick_breaking(qkv):
    b, p, _ = qkv.shape
    npair = D_MODEL // 128
    slot = (SB_GROUP, 2 * CHUNK, CHUNK)
    return pl.pallas_call(
        _sb_kernel,
        grid=(b, npair),
        in_specs=[pl.BlockSpec((1, p, 128), lambda i, h: (i, 0, h)),
                  pl.BlockSpec((1, p, 128), lambda i, h: (i, 0, npair + h)),
                  pl.BlockSpec((1, p, 128), lambda i, h: (i, 0, 2 * npair + h)),
                  _resident((2 * CHUNK, 2 * CHUNK))],
        out_specs=pl.BlockSpec((1, p, 128), lambda i, h: (i, 0, h)),
        out_shape=jax.ShapeDtypeStruct((b, p, D_MODEL), BF16),
        scratch_shapes=[pltpu.VMEM((p // CHUNK, CHUNK, CHUNK), BF16),
                        pltpu.VMEM((SB_GROUP * SB_WINDOW * 2 * CHUNK, 2 * CHUNK), BF16),
                        pltpu.VMEM((SB_GROUP, 2 * CHUNK, SB_WINDOW * CHUNK), BF16),
                        pltpu.VMEM(slot, F32), pltpu.VMEM(slot, F32)],
        compiler_params=_cparams(("parallel", "parallel")),
        name="stick_breaking",
    )(qkv, qkv, qkv, _sb_sum_matrix())


def kernel(x, meta, norm_mix_g, norm_mlp_g, even_w_in, even_ret_gn_g, even_conv_w, even_conv_b,
           even_conv_ln_g, even_conv_ln_b, even_w_out, odd_w_qkv, odd_q_norm_g, odd_k_norm_g,
           odd_w_o, mlp_w1, mlp_w2):
    b, seq, d = x.shape
    depth = norm_mix_g.shape[0]
    p = PAD_FRONT + N_META + seq
    assert d == D_MODEL and p % CHUNK == 0
    assert depth == 2
    m = b * p
    head = jnp.concatenate([jnp.zeros((PAD_FRONT, d), x.dtype), meta.astype(x.dtype)], axis=0)
    h = jnp.concatenate([jnp.broadcast_to(head[None], (b, PAD_FRONT + N_META, d)), x], axis=1)
    h = h.reshape(m, d)
    hn = None
    for layer in range(depth):
        j = layer // 2
        if layer % 2 == 0:
            proj = norm_matmul(h, norm_mix_g[layer], even_w_in[j].astype(BF16)).reshape(b, p, -1)
            o = retention(proj, even_ret_gn_g[j])
            c = conformer_conv(proj, even_conv_w[j], even_conv_b[j], even_conv_ln_g[j], even_conv_ln_b[j])
            w_out = even_w_out[j].astype(BF16)
            n_o = o.shape[-1]
            h, hn = proj_residual([o.reshape(m, -1), c.reshape(m, -1)], [w_out[:n_o], w_out[n_o:]],
                                  h, norm_mlp_g[layer])
        else:
            qkv = qkv_proj(hn, odd_w_qkv[j].astype(BF16), odd_q_norm_g[j], odd_k_norm_g[j])
            o = stick_breaking(qkv.reshape(b, p, -1))
            h, hn = proj_residual([o.reshape(m, -1)], [odd_w_o[j].astype(BF16)], h, norm_mlp_g[layer])
        g_next = norm_mix_g[layer + 1] if layer + 1 < depth else None
        h, hn = mlp_residual(hn, mlp_w1[layer].astype(BF16), mlp_w2[layer].astype(BF16), h, g_next)
    return h.reshape(b, p, d)[:, PAD_FRONT + N_META:]
```

```python
import functools

import numpy as np
import jax
import jax.numpy as jnp
from jax import lax
from jax.experimental import pallas as pl
from jax.experimental.pallas import tpu as pltpu

F32 = jnp.float32
BF16 = jnp.bfloat16

D_MODEL = 1024
N_META = 16
CHUNK = 128
PAD_FRONT = (-N_META) % CHUNK
RET_HEADS = 4
RET_QK_DIM = 128
RET_V_DIM = 256
CONV_WIDTH = 31
CONV_HALO = 32
SUBLANES = 8
RET_DECAY_OFFSET = 5.0
ROPE_BASE = 10000.0
SB_HEADS = 16
SB_HEAD_DIM = 64
D_FF = 4 * D_MODEL
EPS = 1e-6
SB_EXIT_LOG = -104.0
VMEM_LIMIT = 48 * 1024 * 1024


def _pick(m, candidates):
    for c in candidates:
        if m % c == 0:
            return c
    raise ValueError(f"no tile in {candidates} divides {m}")


def _cparams(sem):
    return pltpu.CompilerParams(dimension_semantics=sem, vmem_limit_bytes=VMEM_LIMIT)


def _resident(shape):
    return pl.BlockSpec(shape, lambda *_: (0,) * len(shape), pipeline_mode=pl.Buffered(1))


def _rms(xf, g):
    return xf * lax.rsqrt(jnp.mean(xf * xf, axis=-1, keepdims=True) + EPS) * g


def _norm_matmul_kernel(x_ref, g_ref, w_ref, o_ref, *, n_chunks):
    hn = _rms(x_ref[...], g_ref[...]).astype(BF16)
    tn = w_ref.shape[1] // n_chunks
    for c in range(n_chunks):
        cols = slice(c * tn, (c + 1) * tn)
        o_ref[:, cols] = jnp.dot(hn, w_ref[:, cols], preferred_element_type=F32).astype(o_ref.dtype)


def norm_matmul(x, g, w):
    m, d = x.shape
    n = w.shape[1]
    tm = _pick(m, (512, 256, 128))
    return pl.pallas_call(
        functools.partial(_norm_matmul_kernel, n_chunks=n // _pick(n, (1024, 512))),
        grid=(m // tm,),
        in_specs=[pl.BlockSpec((tm, d), lambda i: (i, 0)), _resident((1, d)), _resident(w.shape)],
        out_specs=pl.BlockSpec((tm, n), lambda i: (i, 0)),
        out_shape=jax.ShapeDtypeStruct((m, n), BF16),
        compiler_params=_cparams(("parallel",)),
        name="norm_matmul",
    )(x, g.reshape(1, d), w)


def _head_rms(x, g):
    x2 = x * x
    lo = lax.broadcasted_iota(jnp.int32, x.shape, 1) < SB_HEAD_DIM
    s_lo = jnp.sum(jnp.where(lo, x2, 0.0), axis=-1, keepdims=True)
    s_hi = jnp.sum(jnp.where(lo, 0.0, x2), axis=-1, keepdims=True)
    ms = jnp.where(lo, s_lo, s_hi) * (1.0 / SB_HEAD_DIM)
    return x * lax.rsqrt(ms + EPS) * g


def _qkv_kernel(a_ref, w_ref, g_ref, o_ref):
    a = a_ref[...]
    d = a.shape[1]
    for part in range(3):
        y = jnp.dot(a, w_ref[:, part * d:(part + 1) * d], preferred_element_type=F32)
        if part == 2:
            o_ref[:, part * d:(part + 1) * d] = y.astype(o_ref.dtype)
        else:
            g = g_ref[part:part + 1, :]
            for c in range(d // 128):
                o_ref[:, part * d + c * 128:part * d + (c + 1) * 128] = (
                    _head_rms(y[:, c * 128:(c + 1) * 128], g).astype(o_ref.dtype))


def qkv_proj(a, w, qg, kg):
    m, k = a.shape
    n = w.shape[1]
    tm = _pick(m, (512, 256, 128))
    scale = SB_HEAD_DIM ** -0.5
    g = jnp.stack([jnp.tile(qg.astype(F32) * scale, 2), jnp.tile(kg.astype(F32), 2)])
    return pl.pallas_call(
        _qkv_kernel,
        grid=(m // tm,),
        in_specs=[pl.BlockSpec((tm, k), lambda i: (i, 0)), _resident(w.shape), _resident(g.shape)],
        out_specs=pl.BlockSpec((tm, n), lambda i: (i, 0)),
        out_shape=jax.ShapeDtypeStruct((m, n), BF16),
        compiler_params=_cparams(("parallel",)),
        name="qkv_proj",
    )(a, w, g)


def _proj_res_kernel(*refs, n_in, n_sub):
    def rows_of(k):
        parts = [r[...] for r in refs[k * n_sub:(k + 1) * n_sub]]
        return parts[0] if n_sub == 1 else jnp.concatenate(parts, axis=0)

    rest = refs[(n_in + 1) * n_sub:]
    w_refs = rest[:n_in]
    g_ref, ho_ref, hno_ref = rest[n_in:]
    acc = rows_of(n_in)
    for k, w_ref in enumerate(w_refs):
        acc = acc + jnp.dot(rows_of(k), w_ref[...], preferred_element_type=F32)
    ho_ref[...] = acc
    hno_ref[...] = _rms(acc, g_ref[...]).astype(BF16)


def proj_residual(a_list, w_list, h, g_next, unpad=None):
    m, d = h.shape
    if unpad is None:
        tm = _pick(m, (512, 256, 128))
        grid = (m // tm,)
        m_out = m
        n_sub = 1
        row_specs = lambda width: [pl.BlockSpec((tm, width), lambda i: (i, 0))]
        out_spec = pl.BlockSpec((tm, d), lambda i: (i, 0))
    else:
        batch, p, skip = unpad
        assert m == batch * p and p % CHUNK == 0 and skip % CHUNK == 0
        tm = _pick(p - skip, (512, 256, 128))
        tiles = (p - skip) // tm
        grid = (batch, tiles)
        m_out = batch * (p - skip)
        n_sub = tm // CHUNK
        first = lambda i, t: i * (p // CHUNK) + skip // CHUNK + t * n_sub
        row_specs = lambda width: [pl.BlockSpec((CHUNK, width), lambda i, t, k=k: (first(i, t) + k, 0))
                                   for k in range(n_sub)]
        out_spec = pl.BlockSpec((tm, d), lambda i, t: (i * tiles + t, 0))
    row_ops = [*a_list, h]
    in_specs = [spec for x in row_ops for spec in row_specs(x.shape[1])]
    in_specs += [_resident(w.shape) for w in w_list] + [_resident((1, d))]
    return pl.pallas_call(
        functools.partial(_proj_res_kernel, n_in=len(a_list), n_sub=n_sub),
        grid=grid,
        in_specs=in_specs,
        out_specs=[out_spec, out_spec],
        out_shape=[jax.ShapeDtypeStruct((m_out, d), F32), jax.ShapeDtypeStruct((m_out, d), BF16)],
        compiler_params=_cparams(("parallel",) * len(grid)),
        name="proj_residual",
    )(*[x for x in row_ops for _ in range(n_sub)], *w_list, g_next.reshape(1, d))


def _mlp_kernel(hn_ref, w1_ref, w2_ref, h_ref, g_ref, *out_refs, n_chunks, with_norm):
    hn = hn_ref[...]
    tf = w1_ref.shape[1] // n_chunks
    acc = h_ref[...]
    for c in range(n_chunks):
        a = jnp.dot(hn, w1_ref[:, c * tf:(c + 1) * tf], preferred_element_type=F32)
        a = jnp.square(jnp.maximum(a, 0.0)).astype(BF16)
        acc = acc + jnp.dot(a, w2_ref[c * tf:(c + 1) * tf, :], preferred_element_type=F32)
    out_refs[0][...] = acc
    if with_norm:
        out_refs[1][...] = _rms(acc, g_ref[...]).astype(BF16)


def mlp_residual(hn, w1, w2, h, g_next):
    m, d = h.shape
    ff = w1.shape[1]
    tm = _pick(m, (512, 256, 128))
    with_norm = g_next is not None
    g = (g_next if with_norm else jnp.ones((d,), F32)).reshape(1, d)
    row_spec = pl.BlockSpec((tm, d), lambda i: (i, 0))
    out_specs = [row_spec]
    out_shape = [jax.ShapeDtypeStruct((m, d), F32)]
    if with_norm:
        out_specs.append(row_spec)
        out_shape.append(jax.ShapeDtypeStruct((m, d), BF16))
    res = pl.pallas_call(
        functools.partial(_mlp_kernel, n_chunks=ff // _pick(ff, (512,)), with_norm=with_norm),
        grid=(m // tm,),
        in_specs=[row_spec, _resident(w1.shape), _resident(w2.shape), row_spec, _resident((1, d))],
        out_specs=out_specs,
        out_shape=out_shape,
        compiler_params=_cparams(("parallel",)),
        name="mlp",
    )(hn, w1, w2, h, g)
    return res if with_norm else (res[0], None)


def _swap_halves(x):
    half = x.shape[-1] // 2
    return jnp.concatenate([x[:, half:], x[:, :half]], axis=-1)


def _retention_kernel(q_ref, k_ref, v_ref, g_ref, cos_ref, sin_ref, dec_ref, qd_ref, kd_ref,
                      gn_ref, o_ref, s_ref):
    @pl.when(pl.program_id(1) == 0)
    def _():
        s_ref[...] = jnp.zeros_like(s_ref)

    def body(n, carry):
        rows = pl.ds(pl.multiple_of(n * CHUNK, CHUNK), CHUNK)
        cos = cos_ref[rows, :]
        sin = sin_ref[rows, :]
        for h in range(RET_HEADS):
            qk_cols = slice(h * RET_QK_DIM, (h + 1) * RET_QK_DIM)
            v_cols = slice(h * RET_V_DIM, (h + 1) * RET_V_DIM)
            qd = qd_ref[h]
            q = q_ref[0, rows, qk_cols].astype(F32)
            k = k_ref[0, rows, qk_cols].astype(F32)
            v = v_ref[0, rows, v_cols]
            qr = q * cos + _swap_halves(q) * sin
            kr = (k * cos + _swap_halves(k) * sin) * (RET_QK_DIM ** -0.5)
            scores = lax.dot_general(qr.astype(BF16), kr.astype(BF16), (((1,), (1,)), ((), ())),
                                     preferred_element_type=F32) * dec_ref[h]
            o = jnp.dot(scores.astype(BF16), v, preferred_element_type=F32)
            state = s_ref[h]
            o = o + jnp.dot((qr * qd).astype(BF16), state.astype(BF16), preferred_element_type=F32)
            kv = lax.dot_general((kr * kd_ref[h]).astype(BF16), v, (((0,), (0,)), ((), ())),
                                 preferred_element_type=F32)
            s_ref[h] = qd[CHUNK - 1:CHUNK, 0:1] * state + kv
            mu = jnp.mean(o, axis=-1, keepdims=True)
            oc = o - mu
            var = jnp.mean(oc * oc, axis=-1, keepdims=True)
            on = oc * lax.rsqrt(var + EPS) * gn_ref[h]
            gate = g_ref[0, rows, v_cols].astype(F32)
            o_ref[0, rows, v_cols] = (gate * jax.nn.sigmoid(gate) * on).astype(o_ref.dtype)
        return carry

    lax.fori_loop(0, q_ref.shape[1] // CHUNK, body, 0)


def _retention_tables(p):
    half = RET_QK_DIM // 2
    inv_freq = ROPE_BASE ** (-np.arange(half, dtype=np.float64) / half)
    ang = np.arange(p, dtype=np.float64)[:, None] * inv_freq[None, :]
    cos, sin = np.cos(ang), np.sin(ang)
    cos_t = np.concatenate([cos, cos], axis=1)
    sin_t = np.concatenate([-sin, sin], axis=1)
    log_g = np.log1p(-np.exp2(-RET_DECAY_OFFSET - np.arange(RET_HEADS, dtype=np.float64)))
    idx = np.arange(CHUNK, dtype=np.float64)
    diff = idx[:, None] - idx[None, :]
    dec = np.where(diff[None] >= 0, np.exp(np.maximum(diff, 0.0)[None] * log_g[:, None, None]), 0.0)
    qd = np.exp((idx + 1.0)[None, :] * log_g[:, None])
    kd = np.exp((CHUNK - 1 - idx)[None, :] * log_g[:, None])
    bc = lambda t: np.broadcast_to(t[:, :, None], (RET_HEADS, CHUNK, RET_QK_DIM))
    f32 = lambda t: jnp.asarray(np.asarray(t, dtype=np.float32))
    return f32(cos_t), f32(sin_t), f32(dec), f32(bc(qd)), f32(bc(kd))


def retention(proj, gn_g):
    b, p, _ = proj.shape
    cos_t, sin_t, dec, qd, kd = _retention_tables(p)
    qk_w = RET_HEADS * RET_QK_DIM
    v_w = RET_HEADS * RET_V_DIM
    tt = CHUNK * _pick(p // CHUNK, (11, 3, 1))
    return pl.pallas_call(
        _retention_kernel,
        grid=(b, p // tt),
        in_specs=[pl.BlockSpec((1, tt, qk_w), lambda i, t: (i, t, 0)),
                  pl.BlockSpec((1, tt, qk_w), lambda i, t: (i, t, 1)),
                  pl.BlockSpec((1, tt, v_w), lambda i, t: (i, t, (2 * qk_w) // v_w)),
                  pl.BlockSpec((1, tt, v_w), lambda i, t: (i, t, (2 * qk_w) // v_w + 1)),
                  pl.BlockSpec((tt, RET_QK_DIM), lambda i, t: (t, 0)),
                  pl.BlockSpec((tt, RET_QK_DIM), lambda i, t: (t, 0)),
                  _resident(dec.shape), _resident(qd.shape), _resident(kd.shape),
                  _resident((RET_HEADS, 1, RET_V_DIM))],
        out_specs=pl.BlockSpec((1, tt, v_w), lambda i, t: (i, t, 0)),
        out_shape=jax.ShapeDtypeStruct((b, p, v_w), BF16),
        scratch_shapes=[pltpu.VMEM((RET_HEADS, RET_QK_DIM, RET_V_DIM), F32)],
        compiler_params=_cparams(("parallel", "arbitrary")),
        name="retention",
    )(proj, proj, proj, proj, cos_t, sin_t, dec, qd, kd,
      gn_g.astype(F32).reshape(RET_HEADS, 1, RET_V_DIM))


CONV_ROWS = 64


def _conv_kernel(a_ref, gate_ref, w_ref, b_ref, lg_ref, lb_ref, o_ref, sh_ref, y_ref):
    t = pl.program_id(1)
    tt = a_ref.shape[1]
    ch = a_ref.shape[2]
    n_sh = tt + CONV_HALO - SUBLANES

    @pl.when(t == 0)
    def _():
        sh_ref[0, 0:CONV_HALO, :] = jnp.zeros((CONV_HALO, ch), F32)

    @pl.when(t > 0)
    def _():
        sh_ref[0, 0:CONV_HALO, :] = sh_ref[0, tt:tt + CONV_HALO, :]

    gate = gate_ref[0].astype(F32)
    sh_ref[0, CONV_HALO:CONV_HALO + tt, :] = a_ref[0].astype(F32) * jax.nn.sigmoid(gate)
    for r in range(1, SUBLANES):
        sh_ref[r, 0:n_sh, :] = sh_ref[0, r:r + n_sh, :]

    first = CONV_HALO - (CONV_WIDTH - 1)

    def row_block(rb, carry):
        base = pl.multiple_of(rb * CONV_ROWS, CONV_ROWS)
        for c in range(ch // 128):
            cols = slice(c * 128, (c + 1) * 128)
            accs = [jnp.broadcast_to(b_ref[:, cols], (SUBLANES, 128))] * (CONV_ROWS // SUBLANES)
            for j in range(CONV_WIDTH):
                blk, r = divmod(first + j, SUBLANES)
                wj = jnp.broadcast_to(w_ref[j:j + 1, cols], (SUBLANES, 128))
                for k in range(len(accs)):
                    x = sh_ref[r, pl.ds(base + (blk + k) * SUBLANES, SUBLANES), cols]
                    accs[k] = accs[k] + wj * x
            for k, acc in enumerate(accs):
                y_ref[k * SUBLANES:(k + 1) * SUBLANES, cols] = acc
        y = y_ref[...]
        mu = jnp.mean(y, axis=-1, keepdims=True)
        yc = y - mu
        var = jnp.mean(yc * yc, axis=-1, keepdims=True)
        z = yc * lax.rsqrt(var + EPS) * lg_ref[...] + lb_ref[...]
        o_ref[0, pl.ds(base, CONV_ROWS), :] = (z * jax.nn.sigmoid(z)).astype(o_ref.dtype)
        return carry

    lax.fori_loop(0, tt // CONV_ROWS, row_block, 0)


def conformer_conv(proj, conv_w, conv_b, ln_g, ln_b):
    b, p, n = proj.shape
    ch = D_MODEL
    tt = _pick(p, (384, 128))
    a_blk = (n - 2 * ch) // ch
    row = lambda x: x.astype(F32).reshape(1, ch)
    return pl.pallas_call(
        _conv_kernel,
        grid=(b, p // tt),
        in_specs=[pl.BlockSpec((1, tt, ch), lambda i, t: (i, t, a_blk)),
                  pl.BlockSpec((1, tt, ch), lambda i, t: (i, t, a_blk + 1)),
                  _resident((CONV_WIDTH, ch)), _resident((1, ch)), _resident((1, ch)), _resident((1, ch))],
        out_specs=pl.BlockSpec((1, tt, ch), lambda i, t: (i, t, 0)),
        out_shape=jax.ShapeDtypeStruct((b, p, ch), BF16),
        scratch_shapes=[pltpu.VMEM((SUBLANES, CONV_HALO + tt, ch), F32),
                        pltpu.VMEM((CONV_ROWS, ch), F32)],
        compiler_params=_cparams(("parallel", "arbitrary")),
        name="conformer_conv",
    )(proj, proj, conv_w.astype(F32), row(conv_b), row(ln_g), row(ln_b))


SB_WINDOW = 3
SB_GROUP = 3
SB_TOP = 64
SB_FAST_TILES = 5
LOG2E = 1.4426950408889634


def _softplus(z):
    return jnp.maximum(z, 0.0) + jnp.log(1.0 + jnp.exp2(jnp.abs(z) * -LOG2E))


def _split_bf16(x):
    hi = x.astype(BF16)
    return hi, (x - hi.astype(F32)).astype(BF16)


def _sb_first_group(n_blocks):
    n_full = max(n_blocks - SB_WINDOW, 0) // SB_GROUP
    return n_blocks - n_full * SB_GROUP


def _sb_kernel(q_ref, k_ref, v_ref, u_ref, o_ref, kt_ref, hl_ref, w_ref, w3_ref, acc_ref, csum_ref):
    n_blocks = q_ref.shape[1] // CHUNK
    n_first = _sb_first_group(n_blocks)
    lane = lax.broadcasted_iota(jnp.int32, (CHUNK, CHUNK), 1)
    rowi = lax.broadcasted_iota(jnp.int32, (CHUNK, CHUNK), 0)
    head0 = lane < SB_HEAD_DIM
    tri = lane < rowi
    lane2 = lax.broadcasted_iota(jnp.int32, (2 * CHUNK, CHUNK), 1)
    row2 = lax.broadcasted_iota(jnp.int32, (2 * CHUNK, CHUNK), 0) & (CHUNK - 1)

    def rows(j):
        if isinstance(j, int):
            return slice(j * CHUNK, (j + 1) * CHUNK)
        return pl.ds(pl.multiple_of(j * CHUNK, CHUNK), CHUNK)

    for j in range(n_blocks):
        kt_ref[j] = k_ref[0, rows(j), :].astype(F32).T.astype(BF16)

    def stacked_q(i):
        q = q_ref[0, rows(i), :]
        zero = jnp.zeros_like(q)
        return jnp.concatenate([jnp.where(head0, q, zero), jnp.where(head0, zero, q)], axis=0)

    def tile(x, a, blk):
        return x[a * CHUNK:(a + 1) * CHUNK, blk * CHUNK:(blk + 1) * CHUNK]

    def slot_rows(u, a, blk):
        r0 = ((u * SB_WINDOW + blk) * 2 + a) * CHUNK
        return slice(r0, r0 + CHUNK)

    def masked_block(u, qs, i, j):
        key_pos = j * CHUNK + lane2
        valid = (key_pos < i * CHUNK + row2) & (key_pos >= PAD_FRONT)
        z = jnp.dot(qs, kt_ref[j], preferred_element_type=F32)
        sp = jnp.where(valid, _softplus(z), 0.0)
        hi, lo = _split_bf16(sp)
        sums = jnp.dot(jnp.concatenate([hi, lo], axis=1), u_ref[...], preferred_element_type=F32)
        csum = csum_ref[u]
        w = jnp.where(valid, jnp.exp(z - sums[:, :CHUNK] - csum), 0.0)
        acc_ref[u] += jnp.dot(w.astype(BF16), v_ref[0, rows(j), :], preferred_element_type=F32)
        csum_ref[u] = csum + sums[:, CHUNK:]

    def window_scores(u, qs, i, masked):
        js = [i - (SB_WINDOW - 1) + blk for blk in range(SB_WINDOW)]
        jc = [max(j, 0) for j in js] if masked else js
        kt = jnp.concatenate([kt_ref[j] for j in jc], axis=1)
        zs = jnp.dot(qs, kt, preferred_element_type=F32)
        valids = []
        for blk in range(SB_WINDOW):
            if masked:
                key_pos = js[blk] * CHUNK + lane
                valid = (key_pos < i * CHUNK + rowi) & (key_pos >= PAD_FRONT)
            else:
                valid = tri if blk == SB_WINDOW - 1 else None
            valids.append(valid)
            for a in range(2):
                sp = _softplus(tile(zs, a, blk))
                if valid is not None:
                    sp = jnp.where(valid, sp, 0.0)
                hi, lo = _split_bf16(sp)
                hl_ref[slot_rows(u, a, blk), 0:CHUNK] = hi
                hl_ref[slot_rows(u, a, blk), CHUNK:2 * CHUNK] = lo
        return zs, valids, jc

    def window_weights(u, zs, valids, jc, sums):
        for a in range(2):
            csum = None
            for blk in reversed(range(SB_WINDOW)):
                x = tile(zs, a, blk) - sums[slot_rows(u, a, blk), :CHUNK]
                if csum is not None:
                    x = x - csum
                w = jnp.exp(x)
                if valids[blk] is not None:
                    w = jnp.where(valids[blk], w, 0.0)
                w_ref[u, a * CHUNK:(a + 1) * CHUNK, blk * CHUNK:(blk + 1) * CHUNK] = w.astype(BF16)
                tot = sums[slot_rows(u, a, blk), CHUNK:]
                csum = tot if csum is None else csum + tot
            csum_ref[u, a * CHUNK:(a + 1) * CHUNK, :] = csum
        v_cat = jnp.concatenate([v_ref[0, rows(j), :] for j in jc], axis=0)
        acc_ref[u] = jnp.dot(w_ref[u], v_cat, preferred_element_type=F32)

    def fast_scores(u, qs, i):
        kt = jnp.concatenate([kt_ref[i - 1], kt_ref[i]], axis=1)
        zs = jnp.dot(qs, kt, preferred_element_type=F32)
        qs_top = jnp.concatenate([qs[0:SB_TOP], qs[CHUNK:CHUNK + SB_TOP]], axis=0)
        z3 = jnp.dot(qs_top, kt_ref[i - 2], preferred_element_type=F32)
        base = u * SB_FAST_TILES * CHUNK
        for blk in range(2):
            for a in range(2):
                sp = _softplus(tile(zs, a, blk))
                if blk == 1:
                    sp = jnp.where(tri, sp, 0.0)
                hi, lo = _split_bf16(sp)
                r0 = base + (blk * 2 + a) * CHUNK
                hl_ref[r0:r0 + CHUNK, 0:CHUNK] = hi
                hl_ref[r0:r0 + CHUNK, CHUNK:2 * CHUNK] = lo
        hi, lo = _split_bf16(_softplus(z3))
        r0 = base + 4 * CHUNK
        hl_ref[r0:r0 + CHUNK, 0:CHUNK] = hi
        hl_ref[r0:r0 + CHUNK, CHUNK:2 * CHUNK] = lo
        return zs, z3

    def fast_weights(u, zs, z3, i, sums):
        base = u * SB_FAST_TILES * CHUNK
        for a in range(2):
            s_d = sums[base + (2 + a) * CHUNK:base + (3 + a) * CHUNK]
            s_1 = sums[base + a * CHUNK:base + (a + 1) * CHUNK]
            s_3 = sums[base + 4 * CHUNK + a * SB_TOP:base + 4 * CHUNK + (a + 1) * SB_TOP]
            w_d = jnp.where(tri, jnp.exp(tile(zs, a, 1) - s_d[:, :CHUNK]), 0.0)
            csum = s_d[:, CHUNK:]
            w_1 = jnp.exp(tile(zs, a, 0) - s_1[:, :CHUNK] - csum)
            csum = csum + s_1[:, CHUNK:]
            w_3 = jnp.exp(z3[a * SB_TOP:(a + 1) * SB_TOP] - s_3[:, :CHUNK] - csum[0:SB_TOP])
            w_ref[u, a * CHUNK:(a + 1) * CHUNK, 0:CHUNK] = w_1.astype(BF16)
            w_ref[u, a * CHUNK:(a + 1) * CHUNK, CHUNK:2 * CHUNK] = w_d.astype(BF16)
            w3_ref[u, a * SB_TOP:(a + 1) * SB_TOP, :] = w_3.astype(BF16)
            csum_ref[u, a * CHUNK:a * CHUNK + SB_TOP, :] = csum[0:SB_TOP] + s_3[:, CHUNK:]
            csum_ref[u, a * CHUNK + SB_TOP:(a + 1) * CHUNK, :] = csum[SB_TOP:]
        v_cat = jnp.concatenate([v_ref[0, rows(i - 1), :], v_ref[0, rows(i), :]], axis=0)
        acc_ref[u] = jnp.dot(w_ref[u, :, 0:2 * CHUNK], v_cat, preferred_element_type=F32)
        acc3 = jnp.dot(w3_ref[u], v_ref[0, rows(i - 2), :], preferred_element_type=F32)
        for a in range(2):
            acc_ref[u, a * CHUNK:a * CHUNK + SB_TOP, :] += acc3[a * SB_TOP:(a + 1) * SB_TOP]

    def finish(u, qs, i, j_start):
        def cond(j):
            return jnp.logical_and(j >= 0, jnp.min(csum_ref[u]) <= -SB_EXIT_LOG)

        def body(j):
            masked_block(u, qs, i, j)
            return j - 1

        lax.while_loop(cond, body, j_start)

    def write_out(u, i):
        acc = acc_ref[u]
        o_ref[0, rows(i), :] = jnp.where(head0, acc[:CHUNK], acc[CHUNK:]).astype(o_ref.dtype)

    def first_group(n):
        qss = [stacked_q(u) for u in range(n)]
        scored = [window_scores(u, qss[u], u, True) for u in range(n)]
        sums = jnp.dot(hl_ref[0:n * SB_WINDOW * 2 * CHUNK, :], u_ref[...], preferred_element_type=F32)
        for u in range(n):
            window_weights(u, *scored[u], sums)
            finish(u, qss[u], u, u - SB_WINDOW)
            write_out(u, u)

    def fast_group(i0):
        qss = [stacked_q(i0 + u) for u in range(SB_GROUP)]
        scored = [fast_scores(u, qss[u], i0 + u) for u in range(SB_GROUP)]
        sums = jnp.dot(hl_ref[0:SB_GROUP * SB_FAST_TILES * CHUNK, :], u_ref[...],
                       preferred_element_type=F32)
        for u in range(SB_GROUP):
            fast_weights(u, *scored[u], i0 + u, sums)
        low = csum_ref[0]
        for u in range(1, SB_GROUP):
            low = jnp.minimum(low, csum_ref[u])

        @pl.when(jnp.min(low) <= -SB_EXIT_LOG)
        def _():
            for u in range(SB_GROUP):
                redo = jnp.min(csum_ref[u]) <= -SB_EXIT_LOG

                @pl.when(redo)
                def _():
                    acc_ref[u] = jnp.zeros(acc_ref.shape[1:], F32)
                    csum_ref[u] = jnp.zeros(csum_ref.shape[1:], F32)

                finish(u, qss[u], i0 + u, jnp.where(redo, i0 + u, -1))

        for u in range(SB_GROUP):
            write_out(u, i0 + u)

    def group_body(g, c):
        fast_group(n_first + g * SB_GROUP)
        return c

    first_group(n_first)
    lax.fori_loop(0, (n_blocks - n_first) // SB_GROUP, group_body, 0)


def _sb_sum_matrix():
    s = np.arange(CHUNK)
    incl = (s[:, None] >= s[None, :]).astype(np.float32)
    half = np.concatenate([incl, np.ones((CHUNK, CHUNK), np.float32)], axis=1)
    return jnp.asarray(np.concatenate([half, half], axis=0), dtype=BF16)


def stick_breaking(qkv):
    b, p, _ = qkv.shape
    npair = D_MODEL // 128
    n_slots = max(SB_GROUP, _sb_first_group(p // CHUNK))
    slot = (n_slots, 2 * CHUNK, CHUNK)
    return pl.pallas_call(
        _sb_kernel,
        grid=(b, npair),
        in_specs=[pl.BlockSpec((1, p, 128), lambda i, h: (i, 0, h)),
                  pl.BlockSpec((1, p, 128), lambda i, h: (i, 0, npair + h)),
                  pl.BlockSpec((1, p, 128), lambda i, h: (i, 0, 2 * npair + h)),
                  _resident((2 * CHUNK, 2 * CHUNK))],
        out_specs=pl.BlockSpec((1, p, 128), lambda i, h: (i, 0, h)),
        out_shape=jax.ShapeDtypeStruct((b, p, D_MODEL), BF16),
        scratch_shapes=[pltpu.VMEM((p // CHUNK, CHUNK, CHUNK), BF16),
                        pltpu.VMEM((n_slots * SB_WINDOW * 2 * CHUNK, 2 * CHUNK), BF16),
                        pltpu.VMEM((n_slots, 2 * CHUNK, SB_WINDOW * CHUNK), BF16),
                        pltpu.VMEM((SB_GROUP, 2 * SB_TOP, CHUNK), BF16),
                        pltpu.VMEM(slot, F32), pltpu.VMEM(slot, F32)],
        compiler_params=_cparams(("parallel", "parallel")),
        name="stick_breaking",
    )(qkv, qkv, qkv, _sb_sum_matrix())


def kernel(x, meta, norm_mix_g, norm_mlp_g, even_w_in, even_ret_gn_g, even_conv_w, even_conv_b,
           even_conv_ln_g, even_conv_ln_b, even_w_out, odd_w_qkv, odd_q_norm_g, odd_k_norm_g,
           odd_w_o, mlp_w1, mlp_w2):
    b, seq, d = x.shape
    depth = norm_mix_g.shape[0]
    p = PAD_FRONT + N_META + seq
    assert d == D_MODEL and p % CHUNK == 0
    assert depth == 2
    m = b * p
    head = jnp.concatenate([jnp.zeros((PAD_FRONT, d), x.dtype), meta.astype(x.dtype)], axis=0)
    h = jnp.concatenate([jnp.broadcast_to(head[None], (b, PAD_FRONT + N_META, d)), x], axis=1)
    h = h.reshape(m, d)
    hn = None
    for layer in range(depth):
        j = layer // 2
        last = layer + 1 == depth
        unpad = (b, p, PAD_FRONT + N_META) if last else None
        if layer % 2 == 0:
            proj = norm_matmul(h, norm_mix_g[layer], even_w_in[j].astype(BF16)).reshape(b, p, -1)
            o = retention(proj, even_ret_gn_g[j])
            c = conformer_conv(proj, even_conv_w[j], even_conv_b[j], even_conv_ln_g[j], even_conv_ln_b[j])
            w_out = even_w_out[j].astype(BF16)
            n_o = o.shape[-1]
            h, hn = proj_residual([o.reshape(m, -1), c.reshape(m, -1)], [w_out[:n_o], w_out[n_o:]],
                                  h, norm_mlp_g[layer], unpad)
        else:
            qkv = qkv_proj(hn, odd_w_qkv[j].astype(BF16), odd_q_norm_g[j], odd_k_norm_g[j])
            o = stick_breaking(qkv.reshape(b, p, -1))
            h, hn = proj_residual([o.reshape(m, -1)], [odd_w_o[j].astype(BF16)], h, norm_mlp_g[layer], unpad)
        g_next = None if last else norm_mix_g[layer + 1]
        h, hn = mlp_residual(hn, mlp_w1[layer].astype(BF16), mlp_w2[layer].astype(BF16), h, g_next)
    return h.reshape(b, seq, d)
```

```python
import functools

import numpy as np
import jax
import jax.numpy as jnp
from jax import lax
from jax.experimental import pallas as pl
from jax.experimental.pallas import tpu as pltpu

F32 = jnp.float32
BF16 = jnp.bfloat16

D_MODEL = 1024
N_META = 16
CHUNK = 128
PAD_FRONT = (-N_META) % CHUNK
RET_HEADS = 4
RET_QK_DIM = 128
RET_V_DIM = 256
CONV_WIDTH = 31
CONV_HALO = 32
SUBLANES = 8
RET_DECAY_OFFSET = 5.0
ROPE_BASE = 10000.0
SB_HEADS = 16
SB_HEAD_DIM = 64
D_FF = 4 * D_MODEL
EPS = 1e-6
SB_EXIT_LOG = -104.0
VMEM_LIMIT = 48 * 1024 * 1024


def _pick(m, candidates):
    for c in candidates:
        if m % c == 0:
            return c
    raise ValueError(f"no tile in {candidates} divides {m}")


def _cparams(sem):
    return pltpu.CompilerParams(dimension_semantics=sem, vmem_limit_bytes=VMEM_LIMIT)


def _resident(shape):
    return pl.BlockSpec(shape, lambda *_: (0,) * len(shape), pipeline_mode=pl.Buffered(1))


def _rms(xf, g):
    return xf * lax.rsqrt(jnp.mean(xf * xf, axis=-1, keepdims=True) + EPS) * g


def _norm_matmul_kernel(x_ref, g_ref, w_ref, o_ref, *, n_chunks):
    hn = _rms(x_ref[...], g_ref[...]).astype(BF16)
    tn = w_ref.shape[1] // n_chunks
    for c in range(n_chunks):
        cols = slice(c * tn, (c + 1) * tn)
        o_ref[:, cols] = jnp.dot(hn, w_ref[:, cols], preferred_element_type=F32).astype(o_ref.dtype)


def norm_matmul(x, g, w):
    m, d = x.shape
    n = w.shape[1]
    tm = _pick(m, (512, 256, 128))
    return pl.pallas_call(
        functools.partial(_norm_matmul_kernel, n_chunks=n // _pick(n, (1024, 512))),
        grid=(m // tm,),
        in_specs=[pl.BlockSpec((tm, d), lambda i: (i, 0)), _resident((1, d)), _resident(w.shape)],
        out_specs=pl.BlockSpec((tm, n), lambda i: (i, 0)),
        out_shape=jax.ShapeDtypeStruct((m, n), BF16),
        compiler_params=_cparams(("parallel",)),
        name="norm_matmul",
    )(x, g.reshape(1, d), w)


def _head_rms(x, g):
    x2 = x * x
    lo = lax.broadcasted_iota(jnp.int32, x.shape, 1) < SB_HEAD_DIM
    s_lo = jnp.sum(jnp.where(lo, x2, 0.0), axis=-1, keepdims=True)
    s_hi = jnp.sum(jnp.where(lo, 0.0, x2), axis=-1, keepdims=True)
    ms = jnp.where(lo, s_lo, s_hi) * (1.0 / SB_HEAD_DIM)
    return x * lax.rsqrt(ms + EPS) * g


def _qkv_kernel(a_ref, w_ref, g_ref, o_ref):
    a = a_ref[...]
    d = a.shape[1]
    for part in range(3):
        y = jnp.dot(a, w_ref[:, part * d:(part + 1) * d], preferred_element_type=F32)
        if part == 2:
            o_ref[:, part * d:(part + 1) * d] = y.astype(o_ref.dtype)
        else:
            g = g_ref[part:part + 1, :]
            for c in range(d // 128):
                o_ref[:, part * d + c * 128:part * d + (c + 1) * 128] = (
                    _head_rms(y[:, c * 128:(c + 1) * 128], g).astype(o_ref.dtype))


def qkv_proj(a, w, qg, kg):
    m, k = a.shape
    n = w.shape[1]
    tm = _pick(m, (512, 256, 128))
    scale = SB_HEAD_DIM ** -0.5
    g = jnp.stack([jnp.tile(qg.astype(F32) * scale, 2), jnp.tile(kg.astype(F32), 2)])
    return pl.pallas_call(
        _qkv_kernel,
        grid=(m // tm,),
        in_specs=[pl.BlockSpec((tm, k), lambda i: (i, 0)), _resident(w.shape), _resident(g.shape)],
        out_specs=pl.BlockSpec((tm, n), lambda i: (i, 0)),
        out_shape=jax.ShapeDtypeStruct((m, n), BF16),
        compiler_params=_cparams(("parallel",)),
        name="qkv_proj",
    )(a, w, g)


def _proj_res_kernel(*refs, n_in, n_sub):
    def rows_of(k):
        parts = [r[...] for r in refs[k * n_sub:(k + 1) * n_sub]]
        return parts[0] if n_sub == 1 else jnp.concatenate(parts, axis=0)

    rest = refs[(n_in + 1) * n_sub:]
    w_refs = rest[:n_in]
    g_ref, ho_ref, hno_ref = rest[n_in:]
    acc = rows_of(n_in)
    for k, w_ref in enumerate(w_refs):
        acc = acc + jnp.dot(rows_of(k), w_ref[...], preferred_element_type=F32)
    ho_ref[...] = acc
    hno_ref[...] = _rms(acc, g_ref[...]).astype(BF16)


def proj_residual(a_list, w_list, h, g_next, unpad=None):
    m, d = h.shape
    if unpad is None:
        tm = _pick(m, (512, 256, 128))
        grid = (m // tm,)
        m_out = m
        n_sub = 1
        row_specs = lambda width: [pl.BlockSpec((tm, width), lambda i: (i, 0))]
        out_spec = pl.BlockSpec((tm, d), lambda i: (i, 0))
    else:
        batch, p, skip = unpad
        assert m == batch * p and p % CHUNK == 0 and skip % CHUNK == 0
        tm = _pick(p - skip, (512, 256, 128))
        tiles = (p - skip) // tm
        grid = (batch, tiles)
        m_out = batch * (p - skip)
        n_sub = tm // CHUNK
        first = lambda i, t: i * (p // CHUNK) + skip // CHUNK + t * n_sub
        row_specs = lambda width: [pl.BlockSpec((CHUNK, width), lambda i, t, k=k: (first(i, t) + k, 0))
                                   for k in range(n_sub)]
        out_spec = pl.BlockSpec((tm, d), lambda i, t: (i * tiles + t, 0))
    row_ops = [*a_list, h]
    in_specs = [spec for x in row_ops for spec in row_specs(x.shape[1])]
    in_specs += [_resident(w.shape) for w in w_list] + [_resident((1, d))]
    return pl.pallas_call(
        functools.partial(_proj_res_kernel, n_in=len(a_list), n_sub=n_sub),
        grid=grid,
        in_specs=in_specs,
        out_specs=[out_spec, out_spec],
        out_shape=[jax.ShapeDtypeStruct((m_out, d), F32), jax.ShapeDtypeStruct((m_out, d), BF16)],
        compiler_params=_cparams(("parallel",) * len(grid)),
        name="proj_residual",
    )(*[x for x in row_ops for _ in range(n_sub)], *w_list, g_next.reshape(1, d))


def _mlp_kernel(hn_ref, w1_ref, w2_ref, h_ref, g_ref, *out_refs, n_chunks, with_norm):
    hn = hn_ref[...]
    tf = w1_ref.shape[1] // n_chunks
    acc = h_ref[...]
    for c in range(n_chunks):
        a = jnp.dot(hn, w1_ref[:, c * tf:(c + 1) * tf], preferred_element_type=F32)
        a = jnp.square(jnp.maximum(a, 0.0)).astype(BF16)
        acc = acc + jnp.dot(a, w2_ref[c * tf:(c + 1) * tf, :], preferred_element_type=F32)
    out_refs[0][...] = acc
    if with_norm:
        out_refs[1][...] = _rms(acc, g_ref[...]).astype(BF16)


def mlp_residual(hn, w1, w2, h, g_next):
    m, d = h.shape
    ff = w1.shape[1]
    tm = _pick(m, (512, 256, 128))
    with_norm = g_next is not None
    g = (g_next if with_norm else jnp.ones((d,), F32)).reshape(1, d)
    row_spec = pl.BlockSpec((tm, d), lambda i: (i, 0))
    out_specs = [row_spec]
    out_shape = [jax.ShapeDtypeStruct((m, d), F32)]
    if with_norm:
        out_specs.append(row_spec)
        out_shape.append(jax.ShapeDtypeStruct((m, d), BF16))
    res = pl.pallas_call(
        functools.partial(_mlp_kernel, n_chunks=ff // _pick(ff, (512,)), with_norm=with_norm),
        grid=(m // tm,),
        in_specs=[row_spec, _resident(w1.shape), _resident(w2.shape), row_spec, _resident((1, d))],
        out_specs=out_specs,
        out_shape=out_shape,
        compiler_params=_cparams(("parallel",)),
        name="mlp",
    )(hn, w1, w2, h, g)
    return res if with_norm else (res[0], None)


def _swap_halves(x):
    half = x.shape[-1] // 2
    return jnp.concatenate([x[:, half:], x[:, :half]], axis=-1)


def _retention_kernel(q_ref, k_ref, v_ref, g_ref, cos_ref, sin_ref, dec_ref, qd_ref, kd_ref,
                      gn_ref, o_ref, s_ref):
    @pl.when(pl.program_id(1) == 0)
    def _():
        s_ref[...] = jnp.zeros_like(s_ref)

    def body(n, carry):
        rows = pl.ds(pl.multiple_of(n * CHUNK, CHUNK), CHUNK)
        cos = cos_ref[rows, :]
        sin = sin_ref[rows, :]
        for h in range(RET_HEADS):
            qk_cols = slice(h * RET_QK_DIM, (h + 1) * RET_QK_DIM)
            v_cols = slice(h * RET_V_DIM, (h + 1) * RET_V_DIM)
            qd = qd_ref[h]
            q = q_ref[0, rows, qk_cols].astype(F32)
            k = k_ref[0, rows, qk_cols].astype(F32)
            v = v_ref[0, rows, v_cols]
            qr = q * cos + _swap_halves(q) * sin
            kr = (k * cos + _swap_halves(k) * sin) * (RET_QK_DIM ** -0.5)
            scores = lax.dot_general(qr.astype(BF16), kr.astype(BF16), (((1,), (1,)), ((), ())),
                                     preferred_element_type=F32) * dec_ref[h]
            o = jnp.dot(scores.astype(BF16), v, preferred_element_type=F32)
            state = s_ref[h]
            o = o + jnp.dot((qr * qd).astype(BF16), state.astype(BF16), preferred_element_type=F32)
            kv = lax.dot_general((kr * kd_ref[h]).astype(BF16), v, (((0,), (0,)), ((), ())),
                                 preferred_element_type=F32)
            s_ref[h] = qd[CHUNK - 1:CHUNK, 0:1] * state + kv
            mu = jnp.mean(o, axis=-1, keepdims=True)
            oc = o - mu
            var = jnp.mean(oc * oc, axis=-1, keepdims=True)
            on = oc * lax.rsqrt(var + EPS) * gn_ref[h]
            gate = g_ref[0, rows, v_cols].astype(F32)
            o_ref[0, rows, v_cols] = (gate * jax.nn.sigmoid(gate) * on).astype(o_ref.dtype)
        return carry

    lax.fori_loop(0, q_ref.shape[1] // CHUNK, body, 0)


def _retention_tables(p):
    half = RET_QK_DIM // 2
    inv_freq = ROPE_BASE ** (-np.arange(half, dtype=np.float64) / half)
    ang = np.arange(p, dtype=np.float64)[:, None] * inv_freq[None, :]
    cos, sin = np.cos(ang), np.sin(ang)
    cos_t = np.concatenate([cos, cos], axis=1)
    sin_t = np.concatenate([-sin, sin], axis=1)
    log_g = np.log1p(-np.exp2(-RET_DECAY_OFFSET - np.arange(RET_HEADS, dtype=np.float64)))
    idx = np.arange(CHUNK, dtype=np.float64)
    diff = idx[:, None] - idx[None, :]
    dec = np.where(diff[None] >= 0, np.exp(np.maximum(diff, 0.0)[None] * log_g[:, None, None]), 0.0)
    qd = np.exp((idx + 1.0)[None, :] * log_g[:, None])
    kd = np.exp((CHUNK - 1 - idx)[None, :] * log_g[:, None])
    bc = lambda t: np.broadcast_to(t[:, :, None], (RET_HEADS, CHUNK, RET_QK_DIM))
    f32 = lambda t: jnp.asarray(np.asarray(t, dtype=np.float32))
    return f32(cos_t), f32(sin_t), f32(dec), f32(bc(qd)), f32(bc(kd))


def retention(proj, gn_g):
    b, p, _ = proj.shape
    cos_t, sin_t, dec, qd, kd = _retention_tables(p)
    qk_w = RET_HEADS * RET_QK_DIM
    v_w = RET_HEADS * RET_V_DIM
    tt = CHUNK * _pick(p // CHUNK, (11, 3, 1))
    return pl.pallas_call(
        _retention_kernel,
        grid=(b, p // tt),
        in_specs=[pl.BlockSpec((1, tt, qk_w), lambda i, t: (i, t, 0)),
                  pl.BlockSpec((1, tt, qk_w), lambda i, t: (i, t, 1)),
                  pl.BlockSpec((1, tt, v_w), lambda i, t: (i, t, (2 * qk_w) // v_w)),
                  pl.BlockSpec((1, tt, v_w), lambda i, t: (i, t, (2 * qk_w) // v_w + 1)),
                  pl.BlockSpec((tt, RET_QK_DIM), lambda i, t: (t, 0)),
                  pl.BlockSpec((tt, RET_QK_DIM), lambda i, t: (t, 0)),
                  _resident(dec.shape), _resident(qd.shape), _resident(kd.shape),
                  _resident((RET_HEADS, 1, RET_V_DIM))],
        out_specs=pl.BlockSpec((1, tt, v_w), lambda i, t: (i, t, 0)),
        out_shape=jax.ShapeDtypeStruct((b, p, v_w), BF16),
        scratch_shapes=[pltpu.VMEM((RET_HEADS, RET_QK_DIM, RET_V_DIM), F32)],
        compiler_params=_cparams(("parallel", "arbitrary")),
        name="retention",
    )(proj, proj, proj, proj, cos_t, sin_t, dec, qd, kd,
      gn_g.astype(F32).reshape(RET_HEADS, 1, RET_V_DIM))


CONV_ROWS = 64


def _conv_kernel(a_ref, gate_ref, w_ref, b_ref, lg_ref, lb_ref, o_ref, sh_ref, y_ref):
    t = pl.program_id(1)
    tt = a_ref.shape[1]
    ch = a_ref.shape[2]
    n_sh = tt + CONV_HALO - SUBLANES

    @pl.when(t == 0)
    def _():
        sh_ref[0, 0:CONV_HALO, :] = jnp.zeros((CONV_HALO, ch), F32)

    @pl.when(t > 0)
    def _():
        sh_ref[0, 0:CONV_HALO, :] = sh_ref[0, tt:tt + CONV_HALO, :]

    gate = gate_ref[0].astype(F32)
    sh_ref[0, CONV_HALO:CONV_HALO + tt, :] = a_ref[0].astype(F32) * jax.nn.sigmoid(gate)
    for r in range(1, SUBLANES):
        sh_ref[r, 0:n_sh, :] = sh_ref[0, r:r + n_sh, :]

    first = CONV_HALO - (CONV_WIDTH - 1)

    def row_block(rb, carry):
        base = pl.multiple_of(rb * CONV_ROWS, CONV_ROWS)
        for c in range(ch // 128):
            cols = slice(c * 128, (c + 1) * 128)
            accs = [jnp.broadcast_to(b_ref[:, cols], (SUBLANES, 128))] * (CONV_ROWS // SUBLANES)
            for j in range(CONV_WIDTH):
                blk, r = divmod(first + j, SUBLANES)
                wj = jnp.broadcast_to(w_ref[j:j + 1, cols], (SUBLANES, 128))
                for k in range(len(accs)):
                    x = sh_ref[r, pl.ds(base + (blk + k) * SUBLANES, SUBLANES), cols]
                    accs[k] = accs[k] + wj * x
            for k, acc in enumerate(accs):
                y_ref[k * SUBLANES:(k + 1) * SUBLANES, cols] = acc
        y = y_ref[...]
        mu = jnp.mean(y, axis=-1, keepdims=True)
        yc = y - mu
        var = jnp.mean(yc * yc, axis=-1, keepdims=True)
        z = yc * lax.rsqrt(var + EPS) * lg_ref[...] + lb_ref[...]
        o_ref[0, pl.ds(base, CONV_ROWS), :] = (z * jax.nn.sigmoid(z)).astype(o_ref.dtype)
        return carry

    lax.fori_loop(0, tt // CONV_ROWS, row_block, 0)


def conformer_conv(proj, conv_w, conv_b, ln_g, ln_b):
    b, p, n = proj.shape
    ch = D_MODEL
    tt = _pick(p, (384, 128))
    a_blk = (n - 2 * ch) // ch
    row = lambda x: x.astype(F32).reshape(1, ch)
    return pl.pallas_call(
        _conv_kernel,
        grid=(b, p // tt),
        in_specs=[pl.BlockSpec((1, tt, ch), lambda i, t: (i, t, a_blk)),
                  pl.BlockSpec((1, tt, ch), lambda i, t: (i, t, a_blk + 1)),
                  _resident((CONV_WIDTH, ch)), _resident((1, ch)), _resident((1, ch)), _resident((1, ch))],
        out_specs=pl.BlockSpec((1, tt, ch), lambda i, t: (i, t, 0)),
        out_shape=jax.ShapeDtypeStruct((b, p, ch), BF16),
        scratch_shapes=[pltpu.VMEM((SUBLANES, CONV_HALO + tt, ch), F32),
                        pltpu.VMEM((CONV_ROWS, ch), F32)],
        compiler_params=_cparams(("parallel", "arbitrary")),
        name="conformer_conv",
    )(proj, proj, conv_w.astype(F32), row(conv_b), row(ln_g), row(ln_b))


SB_WINDOW = 3
SB_GROUP = 5
LOG2E = 1.4426950408889634


def _softplus(z):
    return jnp.maximum(z, 0.0) + jnp.log(1.0 + jnp.exp2(jnp.abs(z) * -LOG2E))


def _split_bf16(x):
    hi = x.astype(BF16)
    return hi, (x - hi.astype(F32)).astype(BF16)


def _sb_first_group(n_blocks):
    n_full = max(n_blocks - SB_WINDOW, 0) // SB_GROUP
    return n_blocks - n_full * SB_GROUP


def _sb_kernel(q_ref, k_ref, v_ref, u_ref, o_ref, kt_ref, hl_ref, w_ref, acc_ref, csum_ref):
    n_blocks = q_ref.shape[1] // CHUNK
    n_first = _sb_first_group(n_blocks)
    lane = lax.broadcasted_iota(jnp.int32, (CHUNK, CHUNK), 1)
    rowi = lax.broadcasted_iota(jnp.int32, (CHUNK, CHUNK), 0)
    head0 = lane < SB_HEAD_DIM
    tri = lane < rowi
    lane2 = lax.broadcasted_iota(jnp.int32, (2 * CHUNK, CHUNK), 1)
    row2 = lax.broadcasted_iota(jnp.int32, (2 * CHUNK, CHUNK), 0) & (CHUNK - 1)

    def rows(j):
        if isinstance(j, int):
            return slice(j * CHUNK, (j + 1) * CHUNK)
        return pl.ds(pl.multiple_of(j * CHUNK, CHUNK), CHUNK)

    for j in range(n_blocks):
        kt_ref[j] = k_ref[0, rows(j), :].astype(F32).T.astype(BF16)

    def stacked_q(i):
        q = q_ref[0, rows(i), :]
        zero = jnp.zeros_like(q)
        return jnp.concatenate([jnp.where(head0, q, zero), jnp.where(head0, zero, q)], axis=0)

    def tile(x, a, blk):
        return x[a * CHUNK:(a + 1) * CHUNK, blk * CHUNK:(blk + 1) * CHUNK]

    def slot_rows(u, a, blk):
        r0 = ((u * SB_WINDOW + blk) * 2 + a) * CHUNK
        return slice(r0, r0 + CHUNK)

    def masked_block(u, qs, i, j):
        key_pos = j * CHUNK + lane2
        valid = (key_pos < i * CHUNK + row2) & (key_pos >= PAD_FRONT)
        z = jnp.dot(qs, kt_ref[j], preferred_element_type=F32)
        sp = jnp.where(valid, _softplus(z), 0.0)
        hi, lo = _split_bf16(sp)
        sums = jnp.dot(jnp.concatenate([hi, lo], axis=1), u_ref[...], preferred_element_type=F32)
        csum = csum_ref[u]
        w = jnp.where(valid, jnp.exp(z - sums[:, :CHUNK] - csum), 0.0)
        acc_ref[u] += jnp.dot(w.astype(BF16), v_ref[0, rows(j), :], preferred_element_type=F32)
        csum_ref[u] = csum + sums[:, CHUNK:]

    def window_scores(u, qs, i, masked):
        js = [i - (SB_WINDOW - 1) + blk for blk in range(SB_WINDOW)]
        jc = [max(j, 0) for j in js] if masked else js
        kt = jnp.concatenate([kt_ref[j] for j in jc], axis=1)
        zs = jnp.dot(qs, kt, preferred_element_type=F32)
        valids = []
        for blk in range(SB_WINDOW):
            if masked:
                key_pos = js[blk] * CHUNK + lane
                valid = (key_pos < i * CHUNK + rowi) & (key_pos >= PAD_FRONT)
            else:
                valid = tri if blk == SB_WINDOW - 1 else None
            valids.append(valid)
            for a in range(2):
                sp = _softplus(tile(zs, a, blk))
                if valid is not None:
                    sp = jnp.where(valid, sp, 0.0)
                hi, lo = _split_bf16(sp)
                hl_ref[slot_rows(u, a, blk), 0:CHUNK] = hi
                hl_ref[slot_rows(u, a, blk), CHUNK:2 * CHUNK] = lo
        return zs, valids, jc

    def window_weights(u, zs, valids, jc, sums):
        for a in range(2):
            csum = None
            for blk in reversed(range(SB_WINDOW)):
                x = tile(zs, a, blk) - sums[slot_rows(u, a, blk), :CHUNK]
                if csum is not None:
                    x = x - csum
                w = jnp.exp(x)
                if valids[blk] is not None:
                    w = jnp.where(valids[blk], w, 0.0)
                w_ref[u, a * CHUNK:(a + 1) * CHUNK, blk * CHUNK:(blk + 1) * CHUNK] = w.astype(BF16)
                tot = sums[slot_rows(u, a, blk), CHUNK:]
                csum = tot if csum is None else csum + tot
            csum_ref[u, a * CHUNK:(a + 1) * CHUNK, :] = csum
        v_cat = jnp.concatenate([v_ref[0, rows(j), :] for j in jc], axis=0)
        acc_ref[u] = jnp.dot(w_ref[u], v_cat, preferred_element_type=F32)

    def finish(u, qs, i, j_start):
        def cond(j):
            return jnp.logical_and(j >= 0, jnp.min(csum_ref[u]) <= -SB_EXIT_LOG)

        def body(j):
            masked_block(u, qs, i, j)
            return j - 1

        lax.while_loop(cond, body, j_start)

    def write_out(u, i):
        acc = acc_ref[u]
        o_ref[0, rows(i), :] = jnp.where(head0, acc[:CHUNK], acc[CHUNK:]).astype(o_ref.dtype)

    def group(i0, n, masked):
        qss = [stacked_q(i0 + u) for u in range(n)]
        scored = [window_scores(u, qss[u], i0 + u, masked) for u in range(n)]
        sums = jnp.dot(hl_ref[0:n * SB_WINDOW * 2 * CHUNK, :], u_ref[...], preferred_element_type=F32)
        for u in range(n):
            window_weights(u, *scored[u], sums)
        low = csum_ref[0]
        for u in range(1, n):
            low = jnp.minimum(low, csum_ref[u])

        @pl.when(jnp.min(low) <= -SB_EXIT_LOG)
        def _():
            for u in range(n):
                finish(u, qss[u], i0 + u, i0 + u - SB_WINDOW)

        for u in range(n):
            write_out(u, i0 + u)

    def group_body(g, c):
        group(n_first + g * SB_GROUP, SB_GROUP, False)
        return c

    group(0, n_first, True)
    lax.fori_loop(0, (n_blocks - n_first) // SB_GROUP, group_body, 0)


def _sb_sum_matrix():
    s = np.arange(CHUNK)
    incl = (s[:, None] >= s[None, :]).astype(np.float32)
    half = np.concatenate([incl, np.ones((CHUNK, CHUNK), np.float32)], axis=1)
    return jnp.asarray(np.concatenate([half, half], axis=0), dtype=BF16)


def stick_breaking(qkv):
    b, p, _ = qkv.shape
    npair = D_MODEL // 128
    n_slots = max(SB_GROUP, _sb_first_group(p // CHUNK))
    slot = (n_slots, 2 * CHUNK, CHUNK)
    return pl.pallas_call(
        _sb_kernel,
        grid=(b, npair),
        in_specs=[pl.BlockSpec((1, p, 128), lambda i, h: (i, 0, h)),
                  pl.BlockSpec((1, p, 128), lambda i, h: (i, 0, npair + h)),
                  pl.BlockSpec((1, p, 128), lambda i, h: (i, 0, 2 * npair + h)),
                  _resident((2 * CHUNK, 2 * CHUNK))],
        out_specs=pl.BlockSpec((1, p, 128), lambda i, h: (i, 0, h)),
        out_shape=jax.ShapeDtypeStruct((b, p, D_MODEL), BF16),
        scratch_shapes=[pltpu.VMEM((p // CHUNK, CHUNK, CHUNK), BF16),
                        pltpu.VMEM((n_slots * SB_WINDOW * 2 * CHUNK, 2 * CHUNK), BF16),
                        pltpu.VMEM((n_slots, 2 * CHUNK, SB_WINDOW * CHUNK), BF16),
                        pltpu.VMEM(slot, F32), pltpu.VMEM(slot, F32)],
        compiler_params=_cparams(("parallel", "parallel")),
        name="stick_breaking",
    )(qkv, qkv, qkv, _sb_sum_matrix())


def kernel(x, meta, norm_mix_g, norm_mlp_g, even_w_in, even_ret_gn_g, even_conv_w, even_conv_b,
           even_conv_ln_g, even_conv_ln_b, even_w_out, odd_w_qkv, odd_q_norm_g, odd_k_norm_g,
           odd_w_o, mlp_w1, mlp_w2):
    b, seq, d = x.shape
    depth = norm_mix_g.shape[0]
    p = PAD_FRONT + N_META + seq
    assert d == D_MODEL and p % CHUNK == 0
    assert depth == 2
    m = b * p
    head = jnp.concatenate([jnp.zeros((PAD_FRONT, d), x.dtype), meta.astype(x.dtype)], axis=0)
    h = jnp.concatenate([jnp.broadcast_to(head[None], (b, PAD_FRONT + N_META, d)), x], axis=1)
    h = h.reshape(m, d)
    hn = None
    for layer in range(depth):
        j = layer // 2
        last = layer + 1 == depth
        unpad = (b, p, PAD_FRONT + N_META) if last else None
        if layer % 2 == 0:
            proj = norm_matmul(h, norm_mix_g[layer], even_w_in[j].astype(BF16)).reshape(b, p, -1)
            o = retention(proj, even_ret_gn_g[j])
            c = conformer_conv(proj, even_conv_w[j], even_conv_b[j], even_conv_ln_g[j], even_conv_ln_b[j])
            w_out = even_w_out[j].astype(BF16)
            n_o = o.shape[-1]
            h, hn = proj_residual([o.reshape(m, -1), c.reshape(m, -1)], [w_out[:n_o], w_out[n_o:]],
                                  h, norm_mlp_g[layer], unpad)
        else:
            qkv = qkv_proj(hn, odd_w_qkv[j].astype(BF16), odd_q_norm_g[j], odd_k_norm_g[j])
            o = stick_breaking(qkv.reshape(b, p, -1))
            h, hn = proj_residual([o.reshape(m, -1)], [odd_w_o[j].astype(BF16)], h, norm_mlp_g[layer], unpad)
        g_next = None if last else norm_mix_g[layer + 1]
        h, hn = mlp_residual(hn, mlp_w1[layer].astype(BF16), mlp_w2[layer].astype(BF16), h, g_next)
    return h.reshape(b, seq, d)
```

```python
import functools

import numpy as np
import jax
import jax.numpy as jnp
from jax import lax
from jax.experimental import pallas as pl
from jax.experimental.pallas import tpu as pltpu

F32 = jnp.float32
BF16 = jnp.bfloat16

D_MODEL = 1024
N_META = 16
CHUNK = 128
PAD_FRONT = (-N_META) % CHUNK
RET_HEADS = 4
RET_QK_DIM = 128
RET_V_DIM = 256
CONV_WIDTH = 31
CONV_HALO = 32
SUBLANES = 8
RET_DECAY_OFFSET = 5.0
ROPE_BASE = 10000.0
SB_HEADS = 16
SB_HEAD_DIM = 64
D_FF = 4 * D_MODEL
EPS = 1e-6
SB_EXIT_LOG = -104.0
VMEM_LIMIT = 48 * 1024 * 1024


def _pick(m, candidates):
    for c in candidates:
        if m % c == 0:
            return c
    raise ValueError(f"no tile in {candidates} divides {m}")


def _cparams(sem):
    return pltpu.CompilerParams(dimension_semantics=sem, vmem_limit_bytes=VMEM_LIMIT)


def _resident(shape):
    return pl.BlockSpec(shape, lambda *_: (0,) * len(shape), pipeline_mode=pl.Buffered(1))


def _rms(xf, g):
    return xf * lax.rsqrt(jnp.mean(xf * xf, axis=-1, keepdims=True) + EPS) * g


def _norm_matmul_kernel(x_ref, g_ref, w_ref, o_ref, *, n_chunks):
    hn = _rms(x_ref[...], g_ref[...]).astype(BF16)
    tn = w_ref.shape[1] // n_chunks
    for c in range(n_chunks):
        cols = slice(c * tn, (c + 1) * tn)
        o_ref[:, cols] = jnp.dot(hn, w_ref[:, cols], preferred_element_type=F32).astype(o_ref.dtype)


def norm_matmul(x, g, w):
    m, d = x.shape
    n = w.shape[1]
    tm = _pick(m, (512, 256, 128))
    return pl.pallas_call(
        functools.partial(_norm_matmul_kernel, n_chunks=n // _pick(n, (1024, 512))),
        grid=(m // tm,),
        in_specs=[pl.BlockSpec((tm, d), lambda i: (i, 0)), _resident((1, d)), _resident(w.shape)],
        out_specs=pl.BlockSpec((tm, n), lambda i: (i, 0)),
        out_shape=jax.ShapeDtypeStruct((m, n), BF16),
        compiler_params=_cparams(("parallel",)),
        name="norm_matmul",
    )(x, g.reshape(1, d), w)


def _head_rms(x, g):
    x2 = x * x
    lo = lax.broadcasted_iota(jnp.int32, x.shape, 1) < SB_HEAD_DIM
    s_lo = jnp.sum(jnp.where(lo, x2, 0.0), axis=-1, keepdims=True)
    s_hi = jnp.sum(jnp.where(lo, 0.0, x2), axis=-1, keepdims=True)
    ms = jnp.where(lo, s_lo, s_hi) * (1.0 / SB_HEAD_DIM)
    return x * lax.rsqrt(ms + EPS) * g


def _qkv_kernel(a_ref, w_ref, g_ref, o_ref):
    a = a_ref[...]
    d = a.shape[1]
    for part in range(3):
        y = jnp.dot(a, w_ref[:, part * d:(part + 1) * d], preferred_element_type=F32)
        if part == 2:
            o_ref[:, part * d:(part + 1) * d] = y.astype(o_ref.dtype)
        else:
            g = g_ref[part:part + 1, :]
            for c in range(d // 128):
                o_ref[:, part * d + c * 128:part * d + (c + 1) * 128] = (
                    _head_rms(y[:, c * 128:(c + 1) * 128], g).astype(o_ref.dtype))


def qkv_proj(a, w, qg, kg):
    m, k = a.shape
    n = w.shape[1]
    tm = _pick(m, (512, 256, 128))
    scale = SB_HEAD_DIM ** -0.5
    g = jnp.stack([jnp.tile(qg.astype(F32) * scale, 2), jnp.tile(kg.astype(F32), 2)])
    return pl.pallas_call(
        _qkv_kernel,
        grid=(m // tm,),
        in_specs=[pl.BlockSpec((tm, k), lambda i: (i, 0)), _resident(w.shape), _resident(g.shape)],
        out_specs=pl.BlockSpec((tm, n), lambda i: (i, 0)),
        out_shape=jax.ShapeDtypeStruct((m, n), BF16),
        compiler_params=_cparams(("parallel",)),
        name="qkv_proj",
    )(a, w, g)


def _proj_res_kernel(*refs, n_in, n_sub):
    def rows_of(k):
        parts = [r[...] for r in refs[k * n_sub:(k + 1) * n_sub]]
        return parts[0] if n_sub == 1 else jnp.concatenate(parts, axis=0)

    rest = refs[(n_in + 1) * n_sub:]
    w_refs = rest[:n_in]
    g_ref, ho_ref, hno_ref = rest[n_in:]
    acc = rows_of(n_in)
    for k, w_ref in enumerate(w_refs):
        acc = acc + jnp.dot(rows_of(k), w_ref[...], preferred_element_type=F32)
    ho_ref[...] = acc
    hno_ref[...] = _rms(acc, g_ref[...]).astype(BF16)


def proj_residual(a_list, w_list, h, g_next, unpad=None):
    m, d = h.shape
    if unpad is None:
        tm = _pick(m, (512, 256, 128))
        grid = (m // tm,)
        m_out = m
        n_sub = 1
        row_specs = lambda width: [pl.BlockSpec((tm, width), lambda i: (i, 0))]
        out_spec = pl.BlockSpec((tm, d), lambda i: (i, 0))
    else:
        batch, p, skip = unpad
        assert m == batch * p and p % CHUNK == 0 and skip % CHUNK == 0
        tm = _pick(p - skip, (512, 256, 128))
        tiles = (p - skip) // tm
        grid = (batch, tiles)
        m_out = batch * (p - skip)
        n_sub = tm // CHUNK
        first = lambda i, t: i * (p // CHUNK) + skip // CHUNK + t * n_sub
        row_specs = lambda width: [pl.BlockSpec((CHUNK, width), lambda i, t, k=k: (first(i, t) + k, 0))
                                   for k in range(n_sub)]
        out_spec = pl.BlockSpec((tm, d), lambda i, t: (i * tiles + t, 0))
    row_ops = [*a_list, h]
    in_specs = [spec for x in row_ops for spec in row_specs(x.shape[1])]
    in_specs += [_resident(w.shape) for w in w_list] + [_resident((1, d))]
    return pl.pallas_call(
        functools.partial(_proj_res_kernel, n_in=len(a_list), n_sub=n_sub),
        grid=grid,
        in_specs=in_specs,
        out_specs=[out_spec, out_spec],
        out_shape=[jax.ShapeDtypeStruct((m_out, d), F32), jax.ShapeDtypeStruct((m_out, d), BF16)],
        compiler_params=_cparams(("parallel",) * len(grid)),
        name="proj_residual",
    )(*[x for x in row_ops for _ in range(n_sub)], *w_list, g_next.reshape(1, d))


def _mlp_kernel(hn_ref, w1_ref, w2_ref, h_ref, g_ref, *out_refs, n_chunks, with_norm):
    hn = hn_ref[...]
    tf = w1_ref.shape[1] // n_chunks
    acc = h_ref[...]
    for c in range(n_chunks):
        a = jnp.dot(hn, w1_ref[:, c * tf:(c + 1) * tf], preferred_element_type=F32)
        a = jnp.square(jnp.maximum(a, 0.0)).astype(BF16)
        acc = acc + jnp.dot(a, w2_ref[c * tf:(c + 1) * tf, :], preferred_element_type=F32)
    out_refs[0][...] = acc
    if with_norm:
        out_refs[1][...] = _rms(acc, g_ref[...]).astype(BF16)


def mlp_residual(hn, w1, w2, h, g_next):
    m, d = h.shape
    ff = w1.shape[1]
    tm = _pick(m, (512, 256, 128))
    with_norm = g_next is not None
    g = (g_next if with_norm else jnp.ones((d,), F32)).reshape(1, d)
    row_spec = pl.BlockSpec((tm, d), lambda i: (i, 0))
    out_specs = [row_spec]
    out_shape = [jax.ShapeDtypeStruct((m, d), F32)]
    if with_norm:
        out_specs.append(row_spec)
        out_shape.append(jax.ShapeDtypeStruct((m, d), BF16))
    res = pl.pallas_call(
        functools.partial(_mlp_kernel, n_chunks=ff // _pick(ff, (512,)), with_norm=with_norm),
        grid=(m // tm,),
        in_specs=[row_spec, _resident(w1.shape), _resident(w2.shape), row_spec, _resident((1, d))],
        out_specs=out_specs,
        out_shape=out_shape,
        compiler_params=_cparams(("parallel",)),
        name="mlp",
    )(hn, w1, w2, h, g)
    return res if with_norm else (res[0], None)


def _swap_halves(x):
    half = x.shape[-1] // 2
    return jnp.concatenate([x[:, half:], x[:, :half]], axis=-1)


def _retention_kernel(q_ref, k_ref, v_ref, g_ref, cos_ref, sin_ref, dec_ref, qd_ref, kd_ref,
                      gn_ref, o_ref, s_ref):
    @pl.when(pl.program_id(1) == 0)
    def _():
        s_ref[...] = jnp.zeros_like(s_ref)

    def body(n, carry):
        rows = pl.ds(pl.multiple_of(n * CHUNK, CHUNK), CHUNK)
        cos = cos_ref[rows, :]
        sin = sin_ref[rows, :]
        for h in range(RET_HEADS):
            qk_cols = slice(h * RET_QK_DIM, (h + 1) * RET_QK_DIM)
            v_cols = slice(h * RET_V_DIM, (h + 1) * RET_V_DIM)
            qd = qd_ref[h]
            q = q_ref[0, rows, qk_cols].astype(F32)
            k = k_ref[0, rows, qk_cols].astype(F32)
            v = v_ref[0, rows, v_cols]
            qr = q * cos + _swap_halves(q) * sin
            kr = (k * cos + _swap_halves(k) * sin) * (RET_QK_DIM ** -0.5)
            scores = lax.dot_general(qr.astype(BF16), kr.astype(BF16), (((1,), (1,)), ((), ())),
                                     preferred_element_type=F32) * dec_ref[h]
            o = jnp.dot(scores.astype(BF16), v, preferred_element_type=F32)
            state = s_ref[h]
            o = o + jnp.dot((qr * qd).astype(BF16), state.astype(BF16), preferred_element_type=F32)
            kv = lax.dot_general((kr * kd_ref[h]).astype(BF16), v, (((0,), (0,)), ((), ())),
                                 preferred_element_type=F32)
            s_ref[h] = qd[CHUNK - 1:CHUNK, 0:1] * state + kv
            mu = jnp.mean(o, axis=-1, keepdims=True)
            oc = o - mu
            var = jnp.mean(oc * oc, axis=-1, keepdims=True)
            on = oc * lax.rsqrt(var + EPS) * gn_ref[h]
            gate = g_ref[0, rows, v_cols].astype(F32)
            o_ref[0, rows, v_cols] = (gate * jax.nn.sigmoid(gate) * on).astype(o_ref.dtype)
        return carry

    lax.fori_loop(0, q_ref.shape[1] // CHUNK, body, 0, unroll=2)


def _retention_tables(p):
    half = RET_QK_DIM // 2
    inv_freq = ROPE_BASE ** (-np.arange(half, dtype=np.float64) / half)
    ang = np.arange(p, dtype=np.float64)[:, None] * inv_freq[None, :]
    cos, sin = np.cos(ang), np.sin(ang)
    cos_t = np.concatenate([cos, cos], axis=1)
    sin_t = np.concatenate([-sin, sin], axis=1)
    log_g = np.log1p(-np.exp2(-RET_DECAY_OFFSET - np.arange(RET_HEADS, dtype=np.float64)))
    idx = np.arange(CHUNK, dtype=np.float64)
    diff = idx[:, None] - idx[None, :]
    dec = np.where(diff[None] >= 0, np.exp(np.maximum(diff, 0.0)[None] * log_g[:, None, None]), 0.0)
    qd = np.exp((idx + 1.0)[None, :] * log_g[:, None])
    kd = np.exp((CHUNK - 1 - idx)[None, :] * log_g[:, None])
    bc = lambda t: np.broadcast_to(t[:, :, None], (RET_HEADS, CHUNK, RET_QK_DIM))
    f32 = lambda t: jnp.asarray(np.asarray(t, dtype=np.float32))
    return f32(cos_t), f32(sin_t), f32(dec), f32(bc(qd)), f32(bc(kd))


def retention(proj, gn_g):
    b, p, _ = proj.shape
    cos_t, sin_t, dec, qd, kd = _retention_tables(p)
    qk_w = RET_HEADS * RET_QK_DIM
    v_w = RET_HEADS * RET_V_DIM
    tt = CHUNK * _pick(p // CHUNK, (11, 3, 1))
    return pl.pallas_call(
        _retention_kernel,
        grid=(b, p // tt),
        in_specs=[pl.BlockSpec((1, tt, qk_w), lambda i, t: (i, t, 0)),
                  pl.BlockSpec((1, tt, qk_w), lambda i, t: (i, t, 1)),
                  pl.BlockSpec((1, tt, v_w), lambda i, t: (i, t, (2 * qk_w) // v_w)),
                  pl.BlockSpec((1, tt, v_w), lambda i, t: (i, t, (2 * qk_w) // v_w + 1)),
                  pl.BlockSpec((tt, RET_QK_DIM), lambda i, t: (t, 0)),
                  pl.BlockSpec((tt, RET_QK_DIM), lambda i, t: (t, 0)),
                  _resident(dec.shape), _resident(qd.shape), _resident(kd.shape),
                  _resident((RET_HEADS, 1, RET_V_DIM))],
        out_specs=pl.BlockSpec((1, tt, v_w), lambda i, t: (i, t, 0)),
        out_shape=jax.ShapeDtypeStruct((b, p, v_w), BF16),
        scratch_shapes=[pltpu.VMEM((RET_HEADS, RET_QK_DIM, RET_V_DIM), F32)],
        compiler_params=_cparams(("parallel", "arbitrary")),
        name="retention",
    )(proj, proj, proj, proj, cos_t, sin_t, dec, qd, kd,
      gn_g.astype(F32).reshape(RET_HEADS, 1, RET_V_DIM))


CONV_ROWS = 64


def _conv_kernel(a_ref, gate_ref, w_ref, b_ref, lg_ref, lb_ref, o_ref, sh_ref, y_ref):
    t = pl.program_id(1)
    tt = a_ref.shape[1]
    ch = a_ref.shape[2]
    n_sh = tt + CONV_HALO - SUBLANES

    @pl.when(t == 0)
    def _():
        sh_ref[0, 0:CONV_HALO, :] = jnp.zeros((CONV_HALO, ch), F32)

    @pl.when(t > 0)
    def _():
        sh_ref[0, 0:CONV_HALO, :] = sh_ref[0, tt:tt + CONV_HALO, :]

    gate = gate_ref[0].astype(F32)
    sh_ref[0, CONV_HALO:CONV_HALO + tt, :] = a_ref[0].astype(F32) * jax.nn.sigmoid(gate)
    for r in range(1, SUBLANES):
        sh_ref[r, 0:n_sh, :] = sh_ref[0, r:r + n_sh, :]

    first = CONV_HALO - (CONV_WIDTH - 1)

    n_rb = tt // CONV_ROWS
    for c in range(ch // 128):
        cols = slice(c * 128, (c + 1) * 128)
        ws = [jnp.broadcast_to(w_ref[j:j + 1, cols], (SUBLANES, 128)) for j in range(CONV_WIDTH)]
        bias = jnp.broadcast_to(b_ref[:, cols], (SUBLANES, 128))

        def row_block(rb, carry, cols=cols, ws=ws, bias=bias):
            base = pl.multiple_of(rb * CONV_ROWS, CONV_ROWS)
            accs = [bias] * (CONV_ROWS // SUBLANES)
            for j in range(CONV_WIDTH):
                blk, r = divmod(first + j, SUBLANES)
                for k in range(len(accs)):
                    x = sh_ref[r, pl.ds(base + (blk + k) * SUBLANES, SUBLANES), cols]
                    accs[k] = accs[k] + ws[j] * x
            for k, acc in enumerate(accs):
                y_ref[pl.ds(base + k * SUBLANES, SUBLANES), cols] = acc
            return carry

        lax.fori_loop(0, n_rb, row_block, 0, unroll=2)

    for rb in range(n_rb):
        rows = slice(rb * CONV_ROWS, (rb + 1) * CONV_ROWS)
        y = y_ref[rows, :]
        mu = jnp.mean(y, axis=-1, keepdims=True)
        yc = y - mu
        var = jnp.mean(yc * yc, axis=-1, keepdims=True)
        z = yc * lax.rsqrt(var + EPS) * lg_ref[...] + lb_ref[...]
        o_ref[0, rows, :] = (z * jax.nn.sigmoid(z)).astype(o_ref.dtype)


def conformer_conv(proj, conv_w, conv_b, ln_g, ln_b):
    b, p, n = proj.shape
    ch = D_MODEL
    tt = _pick(p, (384, 128))
    a_blk = (n - 2 * ch) // ch
    row = lambda x: x.astype(F32).reshape(1, ch)
    return pl.pallas_call(
        _conv_kernel,
        grid=(b, p // tt),
        in_specs=[pl.BlockSpec((1, tt, ch), lambda i, t: (i, t, a_blk)),
                  pl.BlockSpec((1, tt, ch), lambda i, t: (i, t, a_blk + 1)),
                  _resident((CONV_WIDTH, ch)), _resident((1, ch)), _resident((1, ch)), _resident((1, ch))],
        out_specs=pl.BlockSpec((1, tt, ch), lambda i, t: (i, t, 0)),
        out_shape=jax.ShapeDtypeStruct((b, p, ch), BF16),
        scratch_shapes=[pltpu.VMEM((SUBLANES, CONV_HALO + tt, ch), F32),
                        pltpu.VMEM((tt, ch), F32)],
        compiler_params=_cparams(("parallel", "arbitrary")),
        name="conformer_conv",
    )(proj, proj, conv_w.astype(F32), row(conv_b), row(ln_g), row(ln_b))


SB_WINDOW = 3
SB_GROUP = 5
LOG2E = 1.4426950408889634


def _softplus(z):
    return jnp.maximum(z, 0.0) + jnp.log(1.0 + jnp.exp2(jnp.abs(z) * -LOG2E))


def _split_bf16(x):
    hi = x.astype(BF16)
    return hi, (x - hi.astype(F32)).astype(BF16)


def _sb_first_group(n_blocks):
    n_full = max(n_blocks - SB_WINDOW, 0) // SB_GROUP
    return n_blocks - n_full * SB_GROUP


def _sb_kernel(q_ref, k_ref, v_ref, u_ref, o_ref, kt_ref, hl_ref, w_ref, acc_ref, csum_ref):
    n_blocks = q_ref.shape[1] // CHUNK
    n_first = _sb_first_group(n_blocks)
    lane = lax.broadcasted_iota(jnp.int32, (CHUNK, CHUNK), 1)
    rowi = lax.broadcasted_iota(jnp.int32, (CHUNK, CHUNK), 0)
    head0 = lane < SB_HEAD_DIM
    tri = lane < rowi
    lane2 = lax.broadcasted_iota(jnp.int32, (2 * CHUNK, CHUNK), 1)
    row2 = lax.broadcasted_iota(jnp.int32, (2 * CHUNK, CHUNK), 0) & (CHUNK - 1)

    def rows(j):
        if isinstance(j, int):
            return slice(j * CHUNK, (j + 1) * CHUNK)
        return pl.ds(pl.multiple_of(j * CHUNK, CHUNK), CHUNK)

    for j in range(n_blocks):
        kt_ref[j] = k_ref[0, rows(j), :].astype(F32).T.astype(BF16)

    def stacked_q(i):
        q = q_ref[0, rows(i), :]
        zero = jnp.zeros_like(q)
        return jnp.concatenate([jnp.where(head0, q, zero), jnp.where(head0, zero, q)], axis=0)

    def tile(x, a, blk):
        return x[a * CHUNK:(a + 1) * CHUNK, blk * CHUNK:(blk + 1) * CHUNK]

    def slot_rows(u, a, blk):
        r0 = ((u * SB_WINDOW + blk) * 2 + a) * CHUNK
        return slice(r0, r0 + CHUNK)

    def masked_block(u, qs, i, j):
        key_pos = j * CHUNK + lane2
        valid = (key_pos < i * CHUNK + row2) & (key_pos >= PAD_FRONT)
        z = jnp.dot(qs, kt_ref[j], preferred_element_type=F32)
        sp = jnp.where(valid, _softplus(z), 0.0)
        hi, lo = _split_bf16(sp)
        sums = jnp.dot(jnp.concatenate([hi, lo], axis=1), u_ref[...], preferred_element_type=F32)
        csum = csum_ref[u]
        w = jnp.where(valid, jnp.exp(z - sums[:, :CHUNK] - csum), 0.0)
        acc_ref[u] += jnp.dot(w.astype(BF16), v_ref[0, rows(j), :], preferred_element_type=F32)
        csum_ref[u] = csum + sums[:, CHUNK:]

    def window_scores(u, qs, i, masked):
        js = [i - (SB_WINDOW - 1) + blk for blk in range(SB_WINDOW)]
        jc = [max(j, 0) for j in js] if masked else js
        kt = jnp.concatenate([kt_ref[j] for j in jc], axis=1)
        zs = jnp.dot(qs, kt, preferred_element_type=F32)
        valids = []
        for blk in range(SB_WINDOW):
            if masked:
                key_pos = js[blk] * CHUNK + lane
                valid = (key_pos < i * CHUNK + rowi) & (key_pos >= PAD_FRONT)
            else:
                valid = tri if blk == SB_WINDOW - 1 else None
            valids.append(valid)
            for a in range(2):
                sp = _softplus(tile(zs, a, blk))
                if valid is not None:
                    sp = jnp.where(valid, sp, 0.0)
                hi, lo = _split_bf16(sp)
                hl_ref[slot_rows(u, a, blk), 0:CHUNK] = hi
                hl_ref[slot_rows(u, a, blk), CHUNK:2 * CHUNK] = lo
        return zs, valids, jc

    def window_weights(u, zs, valids, jc, sums):
        for a in range(2):
            csum = None
            for blk in reversed(range(SB_WINDOW)):
                x = tile(zs, a, blk) - sums[slot_rows(u, a, blk), :CHUNK]
                if csum is not None:
                    x = x - csum
                w = jnp.exp(x)
                if valids[blk] is not None:
                    w = jnp.where(valids[blk], w, 0.0)
                w_ref[u, a * CHUNK:(a + 1) * CHUNK, blk * CHUNK:(blk + 1) * CHUNK] = w.astype(BF16)
                tot = sums[slot_rows(u, a, blk), CHUNK:]
                csum = tot if csum is None else csum + tot
            csum_ref[u, a * CHUNK:(a + 1) * CHUNK, :] = csum
        v_cat = jnp.concatenate([v_ref[0, rows(j), :] for j in jc], axis=0)
        acc_ref[u] = jnp.dot(w_ref[u], v_cat, preferred_element_type=F32)

    def finish(u, qs, i, j_start):
        def cond(j):
            return jnp.logical_and(j >= 0, jnp.min(csum_ref[u]) <= -SB_EXIT_LOG)

        def body(j):
            masked_block(u, qs, i, j)
            return j - 1

        lax.while_loop(cond, body, j_start)

    def write_out(u, i):
        acc = acc_ref[u]
        o_ref[0, rows(i), :] = jnp.where(head0, acc[:CHUNK], acc[CHUNK:]).astype(o_ref.dtype)

    def group(i0, n, masked):
        qss = [stacked_q(i0 + u) for u in range(n)]
        scored = [window_scores(u, qss[u], i0 + u, masked) for u in range(n)]
        sums = jnp.dot(hl_ref[0:n * SB_WINDOW * 2 * CHUNK, :], u_ref[...], preferred_element_type=F32)
        for u in range(n):
            window_weights(u, *scored[u], sums)
        low = csum_ref[0]
        for u in range(1, n):
            low = jnp.minimum(low, csum_ref[u])

        @pl.when(jnp.min(low) <= -SB_EXIT_LOG)
        def _():
            for u in range(n):
                finish(u, qss[u], i0 + u, i0 + u - SB_WINDOW)

        for u in range(n):
            write_out(u, i0 + u)

    def group_body(g, c):
        group(n_first + g * SB_GROUP, SB_GROUP, False)
        return c

    group(0, n_first, True)
    lax.fori_loop(0, (n_blocks - n_first) // SB_GROUP, group_body, 0)


def _sb_sum_matrix():
    s = np.arange(CHUNK)
    incl = (s[:, None] >= s[None, :]).astype(np.float32)
    half = np.concatenate([incl, np.ones((CHUNK, CHUNK), np.float32)], axis=1)
    return jnp.asarray(np.concatenate([half, half], axis=0), dtype=BF16)


def stick_breaking(qkv):
    b, p, _ = qkv.shape
    npair = D_MODEL // 128
    n_slots = max(SB_GROUP, _sb_first_group(p // CHUNK))
    slot = (n_slots, 2 * CHUNK, CHUNK)
    return pl.pallas_call(
        _sb_kernel,
        grid=(b, npair),
        in_specs=[pl.BlockSpec((1, p, 128), lambda i, h: (i, 0, h)),
                  pl.BlockSpec((1, p, 128), lambda i, h: (i, 0, npair + h)),
                  pl.BlockSpec((1, p, 128), lambda i, h: (i, 0, 2 * npair + h)),
                  _resident((2 * CHUNK, 2 * CHUNK))],
        out_specs=pl.BlockSpec((1, p, 128), lambda i, h: (i, 0, h)),
        out_shape=jax.ShapeDtypeStruct((b, p, D_MODEL), BF16),
        scratch_shapes=[pltpu.VMEM((p // CHUNK, CHUNK, CHUNK), BF16),
                        pltpu.VMEM((n_slots * SB_WINDOW * 2 * CHUNK, 2 * CHUNK), BF16),
                        pltpu.VMEM((n_slots, 2 * CHUNK, SB_WINDOW * CHUNK), BF16),
                        pltpu.VMEM(slot, F32), pltpu.VMEM(slot, F32)],
        compiler_params=_cparams(("parallel", "parallel")),
        name="stick_breaking",
    )(qkv, qkv, qkv, _sb_sum_matrix())


def kernel(x, meta, norm_mix_g, norm_mlp_g, even_w_in, even_ret_gn_g, even_conv_w, even_conv_b,
           even_conv_ln_g, even_conv_ln_b, even_w_out, odd_w_qkv, odd_q_norm_g, odd_k_norm_g,
           odd_w_o, mlp_w1, mlp_w2):
    b, seq, d = x.shape
    depth = norm_mix_g.shape[0]
    p = PAD_FRONT + N_META + seq
    assert d == D_MODEL and p % CHUNK == 0
    assert depth == 2
    m = b * p
    head = jnp.concatenate([jnp.zeros((PAD_FRONT, d), x.dtype), meta.astype(x.dtype)], axis=0)
    h = jnp.concatenate([jnp.broadcast_to(head[None], (b, PAD_FRONT + N_META, d)), x], axis=1)
    h = h.reshape(m, d)
    hn = None
    for layer in range(depth):
        j = layer // 2
        last = layer + 1 == depth
        unpad = (b, p, PAD_FRONT + N_META) if last else None
        if layer % 2 == 0:
            proj = norm_matmul(h, norm_mix_g[layer], even_w_in[j].astype(BF16)).reshape(b, p, -1)
            o = retention(proj, even_ret_gn_g[j])
            c = conformer_conv(proj, even_conv_w[j], even_conv_b[j], even_conv_ln_g[j], even_conv_ln_b[j])
            w_out = even_w_out[j].astype(BF16)
            n_o = o.shape[-1]
            h, hn = proj_residual([o.reshape(m, -1), c.reshape(m, -1)], [w_out[:n_o], w_out[n_o:]],
                                  h, norm_mlp_g[layer], unpad)
        else:
            qkv = qkv_proj(hn, odd_w_qkv[j].astype(BF16), odd_q_norm_g[j], odd_k_norm_g[j])
            o = stick_breaking(qkv.reshape(b, p, -1))
            h, hn = proj_residual([o.reshape(m, -1)], [odd_w_o[j].astype(BF16)], h, norm_mlp_g[layer], unpad)
        g_next = None if last else norm_mix_g[layer + 1]
        h, hn = mlp_residual(hn, mlp_w1[layer].astype(BF16), mlp_w2[layer].astype(BF16), h, g_next)
    return h.reshape(b, seq, d)
```

```python
import functools

import numpy as np
import jax
import jax.numpy as jnp
from jax import lax
from jax.experimental import pallas as pl
from jax.experimental.pallas import tpu as pltpu

F32 = jnp.float32
BF16 = jnp.bfloat16

D_MODEL = 1024
N_META = 16
CHUNK = 128
PAD_FRONT = (-N_META) % CHUNK
RET_HEADS = 4
RET_QK_DIM = 128
RET_V_DIM = 256
CONV_WIDTH = 31
CONV_HALO = 32
SUBLANES = 8
RET_DECAY_OFFSET = 5.0
ROPE_BASE = 10000.0
SB_HEADS = 16
SB_HEAD_DIM = 64
D_FF = 4 * D_MODEL
EPS = 1e-6
SB_EXIT_LOG = -104.0
VMEM_LIMIT = 48 * 1024 * 1024


def _pick(m, candidates):
    for c in candidates:
        if m % c == 0:
            return c
    raise ValueError(f"no tile in {candidates} divides {m}")


def _cparams(sem):
    return pltpu.CompilerParams(dimension_semantics=sem, vmem_limit_bytes=VMEM_LIMIT)


def _resident(shape):
    return pl.BlockSpec(shape, lambda *_: (0,) * len(shape), pipeline_mode=pl.Buffered(1))


def _rms(xf, g):
    return xf * lax.rsqrt(jnp.mean(xf * xf, axis=-1, keepdims=True) + EPS) * g


def _x_block_of(nb):
    return lambda g: (g // nb) * (nb - 1) + jnp.maximum(g % nb - 1, 0)


def _gather_padded_rows(x_refs, head_ref, nb):
    n_sub = len(x_refs)
    parts = []
    for k, x_ref in enumerate(x_refs):
        is_head = (pl.program_id(0) * n_sub + k) % nb == 0
        parts.append(jnp.where(is_head, head_ref[...], x_ref[...]))
    return jnp.concatenate(parts, axis=0)


def _padded_row_specs(tm, d, nb):
    n_sub = tm // CHUNK
    xblk = _x_block_of(nb)
    return [pl.BlockSpec((CHUNK, d), lambda i, k=k: (xblk(i * n_sub + k), 0)) for k in range(n_sub)]


def _norm_matmul_kernel(*refs, n_sub, nb, n_chunks):
    x_refs = refs[:n_sub]
    head_ref, g_ref, w_ref, o_ref = refs[n_sub:]
    hn = _rms(_gather_padded_rows(x_refs, head_ref, nb), g_ref[...]).astype(BF16)
    tn = w_ref.shape[1] // n_chunks
    for c in range(n_chunks):
        cols = slice(c * tn, (c + 1) * tn)
        o_ref[:, cols] = jnp.dot(hn, w_ref[:, cols], preferred_element_type=F32).astype(o_ref.dtype)


def norm_matmul(x, head, g, w, m, nb):
    d = x.shape[1]
    n = w.shape[1]
    tm = _pick(m, (512, 256, 128))
    n_sub = tm // CHUNK
    return pl.pallas_call(
        functools.partial(_norm_matmul_kernel, n_sub=n_sub, nb=nb, n_chunks=n // _pick(n, (1024, 512))),
        grid=(m // tm,),
        in_specs=_padded_row_specs(tm, d, nb) + [_resident(head.shape), _resident((1, d)), _resident(w.shape)],
        out_specs=pl.BlockSpec((tm, n), lambda i: (i, 0)),
        out_shape=jax.ShapeDtypeStruct((m, n), BF16),
        compiler_params=_cparams(("parallel",)),
        name="norm_matmul",
    )(*[x] * n_sub, head, g.reshape(1, d), w)


def _head_rms(x, g):
    x2 = x * x
    lo = lax.broadcasted_iota(jnp.int32, x.shape, 1) < SB_HEAD_DIM
    s_lo = jnp.sum(jnp.where(lo, x2, 0.0), axis=-1, keepdims=True)
    s_hi = jnp.sum(jnp.where(lo, 0.0, x2), axis=-1, keepdims=True)
    ms = jnp.where(lo, s_lo, s_hi) * (1.0 / SB_HEAD_DIM)
    return x * lax.rsqrt(ms + EPS) * g


def _qkv_kernel(a_ref, w_ref, g_ref, o_ref):
    a = a_ref[...]
    d = a.shape[1]
    for part in range(3):
        y = jnp.dot(a, w_ref[:, part * d:(part + 1) * d], preferred_element_type=F32)
        if part == 2:
            o_ref[:, part * d:(part + 1) * d] = y.astype(o_ref.dtype)
        else:
            g = g_ref[part:part + 1, :]
            for c in range(d // 128):
                o_ref[:, part * d + c * 128:part * d + (c + 1) * 128] = (
                    _head_rms(y[:, c * 128:(c + 1) * 128], g).astype(o_ref.dtype))


def qkv_proj(a, w, qg, kg):
    m, k = a.shape
    n = w.shape[1]
    tm = _pick(m, (512, 256, 128))
    scale = SB_HEAD_DIM ** -0.5
    g = jnp.stack([jnp.tile(qg.astype(F32) * scale, 2), jnp.tile(kg.astype(F32), 2)])
    return pl.pallas_call(
        _qkv_kernel,
        grid=(m // tm,),
        in_specs=[pl.BlockSpec((tm, k), lambda i: (i, 0)), _resident(w.shape), _resident(g.shape)],
        out_specs=pl.BlockSpec((tm, n), lambda i: (i, 0)),
        out_shape=jax.ShapeDtypeStruct((m, n), BF16),
        compiler_params=_cparams(("parallel",)),
        name="qkv_proj",
    )(a, w, g)


def _proj_res_kernel(*refs, pieces, h_nb):
    n_in = len(pieces) - 1
    groups, ofs = [], 0
    for cnt in pieces:
        groups.append(refs[ofs:ofs + cnt])
        ofs += cnt
    rest = refs[ofs:]
    if h_nb is not None:
        head_ref, rest = rest[0], rest[1:]
    w_refs = rest[:n_in]
    g_ref, ho_ref, hno_ref = rest[n_in:n_in + 3]

    def rows_of(parts):
        return parts[0][...] if len(parts) == 1 else jnp.concatenate([r[...] for r in parts], axis=0)

    if h_nb is None:
        acc = rows_of(groups[n_in])
    else:
        acc = _gather_padded_rows(groups[n_in], head_ref, h_nb)
    for k, w_ref in enumerate(w_refs):
        acc = acc + jnp.dot(rows_of(groups[k]), w_ref[...], preferred_element_type=F32)
    ho_ref[...] = acc
    hno_ref[...] = _rms(acc, g_ref[...]).astype(BF16)


def proj_residual(a_list, w_list, h, g_next, unpad=None, h_pad=None):
    m, d = a_list[0].shape[0], h.shape[1]
    if unpad is None:
        tm = _pick(m, (512, 256, 128))
        grid = (m // tm,)
        m_out = m
        row_specs = lambda width: [pl.BlockSpec((tm, width), lambda i: (i, 0))]
        out_spec = pl.BlockSpec((tm, d), lambda i: (i, 0))
    else:
        assert h_pad is None
        batch, p, skip = unpad
        assert m == batch * p and p % CHUNK == 0 and skip % CHUNK == 0
        tm = _pick(p - skip, (512, 256, 128))
        tiles = (p - skip) // tm
        grid = (batch, tiles)
        m_out = batch * (p - skip)
        n_sub = tm // CHUNK
        first = lambda i, t: i * (p // CHUNK) + skip // CHUNK + t * n_sub
        row_specs = lambda width: [pl.BlockSpec((CHUNK, width), lambda i, t, k=k: (first(i, t) + k, 0))
                                   for k in range(n_sub)]
        out_spec = pl.BlockSpec((tm, d), lambda i, t: (i * tiles + t, 0))
    spec_groups = [row_specs(a.shape[1]) for a in a_list]
    extra = []
    if h_pad is None:
        spec_groups.append(row_specs(d))
    else:
        head, nb = h_pad
        spec_groups.append(_padded_row_specs(tm, d, nb))
        extra = [head]
    pieces = tuple(len(s) for s in spec_groups)
    in_specs = [spec for s in spec_groups for spec in s] + [_resident(e.shape) for e in extra]
    in_specs += [_resident(w.shape) for w in w_list] + [_resident((1, d))]
    row_args = [x for x, cnt in zip([*a_list, h], pieces) for _ in range(cnt)]
    return pl.pallas_call(
        functools.partial(_proj_res_kernel, pieces=pieces, h_nb=None if h_pad is None else h_pad[1]),
        grid=grid,
        in_specs=in_specs,
        out_specs=[out_spec, out_spec],
        out_shape=[jax.ShapeDtypeStruct((m_out, d), F32), jax.ShapeDtypeStruct((m_out, d), BF16)],
        compiler_params=_cparams(("parallel",) * len(grid)),
        name="proj_residual",
    )(*row_args, *extra, *w_list, g_next.reshape(1, d))


def _mlp_kernel(hn_ref, w1_ref, w2_ref, h_ref, g_ref, *out_refs, n_chunks, with_norm):
    hn = hn_ref[...]
    tf = w1_ref.shape[1] // n_chunks
    acc = h_ref[...]
    for c in range(n_chunks):
        a = jnp.dot(hn, w1_ref[:, c * tf:(c + 1) * tf], preferred_element_type=F32)
        a = jnp.square(jnp.maximum(a, 0.0)).astype(BF16)
        acc = acc + jnp.dot(a, w2_ref[c * tf:(c + 1) * tf, :], preferred_element_type=F32)
    out_refs[0][...] = acc
    if with_norm:
        out_refs[1][...] = _rms(acc, g_ref[...]).astype(BF16)


def mlp_residual(hn, w1, w2, h, g_next):
    m, d = h.shape
    ff = w1.shape[1]
    tm = _pick(m, (512, 256, 128))
    with_norm = g_next is not None
    g = (g_next if with_norm else jnp.ones((d,), F32)).reshape(1, d)
    row_spec = pl.BlockSpec((tm, d), lambda i: (i, 0))
    out_specs = [row_spec]
    out_shape = [jax.ShapeDtypeStruct((m, d), F32)]
    if with_norm:
        out_specs.append(row_spec)
        out_shape.append(jax.ShapeDtypeStruct((m, d), BF16))
    res = pl.pallas_call(
        functools.partial(_mlp_kernel, n_chunks=ff // _pick(ff, (512,)), with_norm=with_norm),
        grid=(m // tm,),
        in_specs=[row_spec, _resident(w1.shape), _resident(w2.shape), row_spec, _resident((1, d))],
        out_specs=out_specs,
        out_shape=out_shape,
        compiler_params=_cparams(("parallel",)),
        name="mlp",
    )(hn, w1, w2, h, g)
    return res if with_norm else (res[0], None)


def _swap_halves(x):
    half = x.shape[-1] // 2
    return jnp.concatenate([x[:, half:], x[:, :half]], axis=-1)


def _retention_kernel(q_ref, k_ref, v_ref, g_ref, cos_ref, sin_ref, dec_ref, qd_ref, kd_ref,
                      gn_ref, o_ref, s_ref):
    @pl.when(pl.program_id(1) == 0)
    def _():
        s_ref[...] = jnp.zeros_like(s_ref)

    def body(n, carry):
        rows = pl.ds(pl.multiple_of(n * CHUNK, CHUNK), CHUNK)
        cos = cos_ref[rows, :]
        sin = sin_ref[rows, :]
        for h in range(RET_HEADS):
            qk_cols = slice(h * RET_QK_DIM, (h + 1) * RET_QK_DIM)
            v_cols = slice(h * RET_V_DIM, (h + 1) * RET_V_DIM)
            qd = qd_ref[h]
            q = q_ref[0, rows, qk_cols].astype(F32)
            k = k_ref[0, rows, qk_cols].astype(F32)
            v = v_ref[0, rows, v_cols]
            qr = q * cos + _swap_halves(q) * sin
            kr = (k * cos + _swap_halves(k) * sin) * (RET_QK_DIM ** -0.5)
            scores = lax.dot_general(qr.astype(BF16), kr.astype(BF16), (((1,), (1,)), ((), ())),
                                     preferred_element_type=F32) * dec_ref[h]
            o = jnp.dot(scores.astype(BF16), v, preferred_element_type=F32)
            state = s_ref[h]
            o = o + jnp.dot((qr * qd).astype(BF16), state.astype(BF16), preferred_element_type=F32)
            kv = lax.dot_general((kr * kd_ref[h]).astype(BF16), v, (((0,), (0,)), ((), ())),
                                 preferred_element_type=F32)
            s_ref[h] = qd[CHUNK - 1:CHUNK, 0:1] * state + kv
            mu = jnp.mean(o, axis=-1, keepdims=True)
            oc = o - mu
            var = jnp.mean(oc * oc, axis=-1, keepdims=True)
            on = oc * lax.rsqrt(var + EPS) * gn_ref[h]
            gate = g_ref[0, rows, v_cols].astype(F32)
            o_ref[0, rows, v_cols] = (gate * jax.nn.sigmoid(gate) * on).astype(o_ref.dtype)
        return carry

    lax.fori_loop(0, q_ref.shape[1] // CHUNK, body, 0, unroll=2)


def _retention_tables(p):
    half = RET_QK_DIM // 2
    inv_freq = ROPE_BASE ** (-np.arange(half, dtype=np.float64) / half)
    ang = np.arange(p, dtype=np.float64)[:, None] * inv_freq[None, :]
    cos, sin = np.cos(ang), np.sin(ang)
    cos_t = np.concatenate([cos, cos], axis=1)
    sin_t = np.concatenate([-sin, sin], axis=1)
    log_g = np.log1p(-np.exp2(-RET_DECAY_OFFSET - np.arange(RET_HEADS, dtype=np.float64)))
    idx = np.arange(CHUNK, dtype=np.float64)
    diff = idx[:, None] - idx[None, :]
    dec = np.where(diff[None] >= 0, np.exp(np.maximum(diff, 0.0)[None] * log_g[:, None, None]), 0.0)
    qd = np.exp((idx + 1.0)[None, :] * log_g[:, None])
    kd = np.exp((CHUNK - 1 - idx)[None, :] * log_g[:, None])
    bc = lambda t: np.broadcast_to(t[:, :, None], (RET_HEADS, CHUNK, RET_QK_DIM))
    f32 = lambda t: jnp.asarray(np.asarray(t, dtype=np.float32))
    return f32(cos_t), f32(sin_t), f32(dec), f32(bc(qd)), f32(bc(kd))


def retention(proj, gn_g):
    b, p, _ = proj.shape
    cos_t, sin_t, dec, qd, kd = _retention_tables(p)
    qk_w = RET_HEADS * RET_QK_DIM
    v_w = RET_HEADS * RET_V_DIM
    tt = CHUNK * _pick(p // CHUNK, (11, 3, 1))
    return pl.pallas_call(
        _retention_kernel,
        grid=(b, p // tt),
        in_specs=[pl.BlockSpec((1, tt, qk_w), lambda i, t: (i, t, 0)),
                  pl.BlockSpec((1, tt, qk_w), lambda i, t: (i, t, 1)),
                  pl.BlockSpec((1, tt, v_w), lambda i, t: (i, t, (2 * qk_w) // v_w)),
                  pl.BlockSpec((1, tt, v_w), lambda i, t: (i, t, (2 * qk_w) // v_w + 1)),
                  pl.BlockSpec((tt, RET_QK_DIM), lambda i, t: (t, 0)),
                  pl.BlockSpec((tt, RET_QK_DIM), lambda i, t: (t, 0)),
                  _resident(dec.shape), _resident(qd.shape), _resident(kd.shape),
                  _resident((RET_HEADS, 1, RET_V_DIM))],
        out_specs=pl.BlockSpec((1, tt, v_w), lambda i, t: (i, t, 0)),
        out_shape=jax.ShapeDtypeStruct((b, p, v_w), BF16),
        scratch_shapes=[pltpu.VMEM((RET_HEADS, RET_QK_DIM, RET_V_DIM), F32)],
        compiler_params=_cparams(("parallel", "arbitrary")),
        name="retention",
    )(proj, proj, proj, proj, cos_t, sin_t, dec, qd, kd,
      gn_g.astype(F32).reshape(RET_HEADS, 1, RET_V_DIM))


CONV_ROWS = 64


def _conv_kernel(a_ref, gate_ref, w_ref, b_ref, lg_ref, lb_ref, o_ref, sh_ref, y_ref):
    t = pl.program_id(1)
    tt = a_ref.shape[1]
    ch = a_ref.shape[2]
    n_sh = tt + CONV_HALO - SUBLANES

    @pl.when(t == 0)
    def _():
        sh_ref[0, 0:CONV_HALO, :] = jnp.zeros((CONV_HALO, ch), F32)

    @pl.when(t > 0)
    def _():
        sh_ref[0, 0:CONV_HALO, :] = sh_ref[0, tt:tt + CONV_HALO, :]

    gate = gate_ref[0].astype(F32)
    sh_ref[0, CONV_HALO:CONV_HALO + tt, :] = a_ref[0].astype(F32) * jax.nn.sigmoid(gate)
    for r in range(1, SUBLANES):
        sh_ref[r, 0:n_sh, :] = sh_ref[0, r:r + n_sh, :]

    first = CONV_HALO - (CONV_WIDTH - 1)

    n_rb = tt // CONV_ROWS
    for c in range(ch // 128):
        cols = slice(c * 128, (c + 1) * 128)
        ws = [jnp.broadcast_to(w_ref[j:j + 1, cols], (SUBLANES, 128)) for j in range(CONV_WIDTH)]
        bias = jnp.broadcast_to(b_ref[:, cols], (SUBLANES, 128))

        def row_block(rb, carry, cols=cols, ws=ws, bias=bias):
            base = pl.multiple_of(rb * CONV_ROWS, CONV_ROWS)
            accs = [bias] * (CONV_ROWS // SUBLANES)
            for j in range(CONV_WIDTH):
                blk, r = divmod(first + j, SUBLANES)
                for k in range(len(accs)):
                    x = sh_ref[r, pl.ds(base + (blk + k) * SUBLANES, SUBLANES), cols]
                    accs[k] = accs[k] + ws[j] * x
            for k, acc in enumerate(accs):
                y_ref[pl.ds(base + k * SUBLANES, SUBLANES), cols] = acc
            return carry

        lax.fori_loop(0, n_rb, row_block, 0, unroll=2)

    for rb in range(n_rb):
        rows = slice(rb * CONV_ROWS, (rb + 1) * CONV_ROWS)
        y = y_ref[rows, :]
        mu = jnp.mean(y, axis=-1, keepdims=True)
        yc = y - mu
        var = jnp.mean(yc * yc, axis=-1, keepdims=True)
        z = yc * lax.rsqrt(var + EPS) * lg_ref[...] + lb_ref[...]
        o_ref[0, rows, :] = (z * jax.nn.sigmoid(z)).astype(o_ref.dtype)


def conformer_conv(proj, conv_w, conv_b, ln_g, ln_b):
    b, p, n = proj.shape
    ch = D_MODEL
    tt = _pick(p, (384, 128))
    a_blk = (n - 2 * ch) // ch
    row = lambda x: x.astype(F32).reshape(1, ch)
    return pl.pallas_call(
        _conv_kernel,
        grid=(b, p // tt),
        in_specs=[pl.BlockSpec((1, tt, ch), lambda i, t: (i, t, a_blk)),
                  pl.BlockSpec((1, tt, ch), lambda i, t: (i, t, a_blk + 1)),
                  _resident((CONV_WIDTH, ch)), _resident((1, ch)), _resident((1, ch)), _resident((1, ch))],
        out_specs=pl.BlockSpec((1, tt, ch), lambda i, t: (i, t, 0)),
        out_shape=jax.ShapeDtypeStruct((b, p, ch), BF16),
        scratch_shapes=[pltpu.VMEM((SUBLANES, CONV_HALO + tt, ch), F32),
                        pltpu.VMEM((tt, ch), F32)],
        compiler_params=_cparams(("parallel", "arbitrary")),
        name="conformer_conv",
    )(proj, proj, conv_w.astype(F32), row(conv_b), row(ln_g), row(ln_b))


SB_WINDOW = 3
SB_GROUP = 5
LOG2E = 1.4426950408889634


def _softplus(z):
    return jnp.maximum(z, 0.0) + jnp.log(1.0 + jnp.exp2(jnp.abs(z) * -LOG2E))


def _split_bf16(x):
    hi = x.astype(BF16)
    return hi, (x - hi.astype(F32)).astype(BF16)


def _sb_first_group(n_blocks):
    n_full = max(n_blocks - SB_WINDOW, 0) // SB_GROUP
    return n_blocks - n_full * SB_GROUP


def _sb_kernel(q_ref, k_ref, v_ref, u_ref, o_ref, kt_ref, hl_ref, w_ref, acc_ref, csum_ref):
    n_blocks = q_ref.shape[1] // CHUNK
    n_first = _sb_first_group(n_blocks)
    lane = lax.broadcasted_iota(jnp.int32, (CHUNK, CHUNK), 1)
    rowi = lax.broadcasted_iota(jnp.int32, (CHUNK, CHUNK), 0)
    head0 = lane < SB_HEAD_DIM
    tri = lane < rowi
    lane2 = lax.broadcasted_iota(jnp.int32, (2 * CHUNK, CHUNK), 1)
    row2 = lax.broadcasted_iota(jnp.int32, (2 * CHUNK, CHUNK), 0) & (CHUNK - 1)

    def rows(j):
        if isinstance(j, int):
            return slice(j * CHUNK, (j + 1) * CHUNK)
        return pl.ds(pl.multiple_of(j * CHUNK, CHUNK), CHUNK)

    for j in range(n_blocks):
        kt_ref[j] = k_ref[0, rows(j), :].astype(F32).T.astype(BF16)

    def stacked_q(i):
        q = q_ref[0, rows(i), :]
        zero = jnp.zeros_like(q)
        return jnp.concatenate([jnp.where(head0, q, zero), jnp.where(head0, zero, q)], axis=0)

    def tile(x, a, blk):
        return x[a * CHUNK:(a + 1) * CHUNK, blk * CHUNK:(blk + 1) * CHUNK]

    def slot_rows(u, a, blk):
        r0 = ((u * SB_WINDOW + blk) * 2 + a) * CHUNK
        return slice(r0, r0 + CHUNK)

    def masked_block(u, qs, i, j):
        key_pos = j * CHUNK + lane2
        valid = (key_pos < i * CHUNK + row2) & (key_pos >= PAD_FRONT)
        z = jnp.dot(qs, kt_ref[j], preferred_element_type=F32)
        sp = jnp.where(valid, _softplus(z), 0.0)
        hi, lo = _split_bf16(sp)
        sums = jnp.dot(jnp.concatenate([hi, lo], axis=1), u_ref[...], preferred_element_type=F32)
        csum = csum_ref[u]
        w = jnp.where(valid, jnp.exp(z - sums[:, :CHUNK] - csum), 0.0)
        acc_ref[u] += jnp.dot(w.astype(BF16), v_ref[0, rows(j), :], preferred_element_type=F32)
        csum_ref[u] = csum + sums[:, CHUNK:]

    def window_scores(u, qs, i, masked):
        js = [i - (SB_WINDOW - 1) + blk for blk in range(SB_WINDOW)]
        jc = [max(j, 0) for j in js] if masked else js
        kt = jnp.concatenate([kt_ref[j] for j in jc], axis=1)
        zs = jnp.dot(qs, kt, preferred_element_type=F32)
        valids = []
        for blk in range(SB_WINDOW):
            if masked:
                key_pos = js[blk] * CHUNK + lane
                valid = (key_pos < i * CHUNK + rowi) & (key_pos >= PAD_FRONT)
            else:
                valid = tri if blk == SB_WINDOW - 1 else None
            valids.append(valid)
            for a in range(2):
                sp = _softplus(tile(zs, a, blk))
                if valid is not None:
                    sp = jnp.where(valid, sp, 0.0)
                hi, lo = _split_bf16(sp)
                hl_ref[slot_rows(u, a, blk), 0:CHUNK] = hi
                hl_ref[slot_rows(u, a, blk), CHUNK:2 * CHUNK] = lo
        return zs, valids, jc

    def window_weights(u, zs, valids, jc, sums):
        for a in range(2):
            csum = None
            for blk in reversed(range(SB_WINDOW)):
                x = tile(zs, a, blk) - sums[slot_rows(u, a, blk), :CHUNK]
                if csum is not None:
                    x = x - csum
                w = jnp.exp(x)
                if valids[blk] is not None:
                    w = jnp.where(valids[blk], w, 0.0)
                w_ref[u, a * CHUNK:(a + 1) * CHUNK, blk * CHUNK:(blk + 1) * CHUNK] = w.astype(BF16)
                tot = sums[slot_rows(u, a, blk), CHUNK:]
                csum = tot if csum is None else csum + tot
            csum_ref[u, a * CHUNK:(a + 1) * CHUNK, :] = csum
        v_cat = jnp.concatenate([v_ref[0, rows(j), :] for j in jc], axis=0)
        acc_ref[u] = jnp.dot(w_ref[u], v_cat, preferred_element_type=F32)

    def finish(u, qs, i, j_start):
        def cond(j):
            return jnp.logical_and(j >= 0, jnp.min(csum_ref[u]) <= -SB_EXIT_LOG)

        def body(j):
            masked_block(u, qs, i, j)
            return j - 1

        lax.while_loop(cond, body, j_start)

    def write_out(u, i):
        acc = acc_ref[u]
        o_ref[0, rows(i), :] = jnp.where(head0, acc[:CHUNK], acc[CHUNK:]).astype(o_ref.dtype)

    def group(i0, n, masked):
        qss = [stacked_q(i0 + u) for u in range(n)]
        scored = [window_scores(u, qss[u], i0 + u, masked) for u in range(n)]
        sums = jnp.dot(hl_ref[0:n * SB_WINDOW * 2 * CHUNK, :], u_ref[...], preferred_element_type=F32)
        for u in range(n):
            window_weights(u, *scored[u], sums)
        low = csum_ref[0]
        for u in range(1, n):
            low = jnp.minimum(low, csum_ref[u])

        @pl.when(jnp.min(low) <= -SB_EXIT_LOG)
        def _():
            for u in range(n):
                finish(u, qss[u], i0 + u, i0 + u - SB_WINDOW)

        for u in range(n):
            write_out(u, i0 + u)

    def group_body(g, c):
        group(n_first + g * SB_GROUP, SB_GROUP, False)
        return c

    group(0, n_first, True)
    lax.fori_loop(0, (n_blocks - n_first) // SB_GROUP, group_body, 0)


def _sb_sum_matrix():
    s = np.arange(CHUNK)
    incl = (s[:, None] >= s[None, :]).astype(np.float32)
    half = np.concatenate([incl, np.ones((CHUNK, CHUNK), np.float32)], axis=1)
    return jnp.asarray(np.concatenate([half, half], axis=0), dtype=BF16)


def stick_breaking(qkv):
    b, p, _ = qkv.shape
    npair = D_MODEL // 128
    n_slots = max(SB_GROUP, _sb_first_group(p // CHUNK))
    slot = (n_slots, 2 * CHUNK, CHUNK)
    return pl.pallas_call(
        _sb_kernel,
        grid=(b, npair),
        in_specs=[pl.BlockSpec((1, p, 128), lambda i, h: (i, 0, h)),
                  pl.BlockSpec((1, p, 128), lambda i, h: (i, 0, npair + h)),
                  pl.BlockSpec((1, p, 128), lambda i, h: (i, 0, 2 * npair + h)),
                  _resident((2 * CHUNK, 2 * CHUNK))],
        out_specs=pl.BlockSpec((1, p, 128), lambda i, h: (i, 0, h)),
        out_shape=jax.ShapeDtypeStruct((b, p, D_MODEL), BF16),
        scratch_shapes=[pltpu.VMEM((p // CHUNK, CHUNK, CHUNK), BF16),
                        pltpu.VMEM((n_slots * SB_WINDOW * 2 * CHUNK, 2 * CHUNK), BF16),
                        pltpu.VMEM((n_slots, 2 * CHUNK, SB_WINDOW * CHUNK), BF16),
                        pltpu.VMEM(slot, F32), pltpu.VMEM(slot, F32)],
        compiler_params=_cparams(("parallel", "parallel")),
        name="stick_breaking",
    )(qkv, qkv, qkv, _sb_sum_matrix())


def kernel(x, meta, norm_mix_g, norm_mlp_g, even_w_in, even_ret_gn_g, even_conv_w, even_conv_b,
           even_conv_ln_g, even_conv_ln_b, even_w_out, odd_w_qkv, odd_q_norm_g, odd_k_norm_g,
           odd_w_o, mlp_w1, mlp_w2):
    b, seq, d = x.shape
    depth = norm_mix_g.shape[0]
    p = PAD_FRONT + N_META + seq
    assert d == D_MODEL and p % CHUNK == 0
    assert depth == 2
    m = b * p
    nb = p // CHUNK
    head = jnp.concatenate([jnp.zeros((PAD_FRONT, d), x.dtype), meta.astype(x.dtype)], axis=0)
    h = x.reshape(b * seq, d)
    hn = None
    for layer in range(depth):
        j = layer // 2
        last = layer + 1 == depth
        unpad = (b, p, PAD_FRONT + N_META) if last else None
        if layer % 2 == 0:
            assert layer == 0
            proj = norm_matmul(h, head, norm_mix_g[layer], even_w_in[j].astype(BF16), m, nb).reshape(b, p, -1)
            o = retention(proj, even_ret_gn_g[j])
            c = conformer_conv(proj, even_conv_w[j], even_conv_b[j], even_conv_ln_g[j], even_conv_ln_b[j])
            w_out = even_w_out[j].astype(BF16)
            n_o = o.shape[-1]
            h, hn = proj_residual([o.reshape(m, -1), c.reshape(m, -1)], [w_out[:n_o], w_out[n_o:]],
                                  h, norm_mlp_g[layer], unpad, h_pad=(head, nb))
        else:
            qkv = qkv_proj(hn, odd_w_qkv[j].astype(BF16), odd_q_norm_g[j], odd_k_norm_g[j])
            o = stick_breaking(qkv.reshape(b, p, -1))
            h, hn = proj_residual([o.reshape(m, -1)], [odd_w_o[j].astype(BF16)], h, norm_mlp_g[layer], unpad)
        g_next = None if last else norm_mix_g[layer + 1]
        h, hn = mlp_residual(hn, mlp_w1[layer].astype(BF16), mlp_w2[layer].astype(BF16), h, g_next)
    return h.reshape(b, seq, d)
```

```python
import functools

import numpy as np
import jax
import jax.numpy as jnp
from jax import lax
from jax.experimental import pallas as pl
from jax.experimental.pallas import tpu as pltpu

F32 = jnp.float32
BF16 = jnp.bfloat16

D_MODEL = 1024
N_META = 16
CHUNK = 128
PAD_FRONT = (-N_META) % CHUNK
RET_HEADS = 4
RET_QK_DIM = 128
RET_V_DIM = 256
CONV_WIDTH = 31
CONV_HALO = 32
SUBLANES = 8
RET_DECAY_OFFSET = 5.0
ROPE_BASE = 10000.0
SB_HEADS = 16
SB_HEAD_DIM = 64
D_FF = 4 * D_MODEL
EPS = 1e-6
SB_EXIT_LOG = -104.0
VMEM_LIMIT = 48 * 1024 * 1024


def _pick(m, candidates):
    for c in candidates:
        if m % c == 0:
            return c
    raise ValueError(f"no tile in {candidates} divides {m}")


def _cparams(sem):
    return pltpu.CompilerParams(dimension_semantics=sem, vmem_limit_bytes=VMEM_LIMIT)


def _resident(shape):
    return pl.BlockSpec(shape, lambda *_: (0,) * len(shape), pipeline_mode=pl.Buffered(1))


def _rms(xf, g):
    return xf * lax.rsqrt(jnp.mean(xf * xf, axis=-1, keepdims=True) + EPS) * g


def _x_block_of(nb):
    return lambda g: (g // nb) * (nb - 1) + jnp.maximum(g % nb - 1, 0)


def _gather_padded_rows(x_refs, head_ref, nb):
    n_sub = len(x_refs)
    parts = []
    for k, x_ref in enumerate(x_refs):
        is_head = (pl.program_id(0) * n_sub + k) % nb == 0
        parts.append(jnp.where(is_head, head_ref[...], x_ref[...]))
    return jnp.concatenate(parts, axis=0)


def _padded_row_specs(tm, d, nb):
    n_sub = tm // CHUNK
    xblk = _x_block_of(nb)
    return [pl.BlockSpec((CHUNK, d), lambda i, k=k: (xblk(i * n_sub + k), 0)) for k in range(n_sub)]


def _norm_matmul_kernel(*refs, n_sub, nb, n_chunks):
    x_refs = refs[:n_sub]
    head_ref, g_ref, w_ref, o_ref = refs[n_sub:]
    hn = _rms(_gather_padded_rows(x_refs, head_ref, nb), g_ref[...]).astype(BF16)
    tn = w_ref.shape[1] // n_chunks
    for c in range(n_chunks):
        cols = slice(c * tn, (c + 1) * tn)
        o_ref[:, cols] = jnp.dot(hn, w_ref[:, cols], preferred_element_type=F32).astype(o_ref.dtype)


def norm_matmul(x, head, g, w, m, nb):
    d = x.shape[1]
    n = w.shape[1]
    tm = _pick(m, (512, 256, 128))
    n_sub = tm // CHUNK
    return pl.pallas_call(
        functools.partial(_norm_matmul_kernel, n_sub=n_sub, nb=nb, n_chunks=n // _pick(n, (1024, 512))),
        grid=(m // tm,),
        in_specs=_padded_row_specs(tm, d, nb) + [_resident(head.shape), _resident((1, d)), _resident(w.shape)],
        out_specs=pl.BlockSpec((tm, n), lambda i: (i, 0)),
        out_shape=jax.ShapeDtypeStruct((m, n), BF16),
        compiler_params=_cparams(("parallel",)),
        name="norm_matmul",
    )(*[x] * n_sub, head, g.reshape(1, d), w)


def _head_rms(x, g):
    x2 = x * x
    lo = lax.broadcasted_iota(jnp.int32, x.shape, 1) < SB_HEAD_DIM
    s_lo = jnp.sum(jnp.where(lo, x2, 0.0), axis=-1, keepdims=True)
    s_hi = jnp.sum(jnp.where(lo, 0.0, x2), axis=-1, keepdims=True)
    ms = jnp.where(lo, s_lo, s_hi) * (1.0 / SB_HEAD_DIM)
    return x * lax.rsqrt(ms + EPS) * g


def _qkv_kernel(a_ref, w_ref, g_ref, o_ref):
    a = a_ref[...]
    d = a.shape[1]
    npair = d // 128
    for part in range(3):
        y = jnp.dot(a, w_ref[:, part * d:(part + 1) * d], preferred_element_type=F32)
        for c in range(npair):
            yc = y[:, c * 128:(c + 1) * 128]
            if part < 2:
                yc = _head_rms(yc, g_ref[part:part + 1, :])
            o_ref[part * npair + c] = yc.astype(o_ref.dtype)


def qkv_proj(a, w, qg, kg):
    m, k = a.shape
    n_slabs = w.shape[1] // 128
    tm = _pick(m, (512, 256, 128))
    scale = SB_HEAD_DIM ** -0.5
    g = jnp.stack([jnp.tile(qg.astype(F32) * scale, 2), jnp.tile(kg.astype(F32), 2)])
    return pl.pallas_call(
        _qkv_kernel,
        grid=(m // tm,),
        in_specs=[pl.BlockSpec((tm, k), lambda i: (i, 0)), _resident(w.shape), _resident(g.shape)],
        out_specs=pl.BlockSpec((n_slabs, tm, 128), lambda i: (0, i, 0)),
        out_shape=jax.ShapeDtypeStruct((n_slabs, m, 128), BF16),
        compiler_params=_cparams(("parallel",)),
        name="qkv_proj",
    )(a, w, g)


def _proj_res_kernel(*refs, pieces, h_nb):
    n_in = len(pieces) - 1
    groups, ofs = [], 0
    for cnt in pieces:
        groups.append(refs[ofs:ofs + cnt])
        ofs += cnt
    rest = refs[ofs:]
    if h_nb is not None:
        head_ref, rest = rest[0], rest[1:]
    w_refs = rest[:n_in]
    g_ref, ho_ref, hno_ref = rest[n_in:n_in + 3]

    def rows_of(parts):
        return parts[0][...] if len(parts) == 1 else jnp.concatenate([r[...] for r in parts], axis=0)

    if h_nb is None:
        acc = rows_of(groups[n_in])
    else:
        acc = _gather_padded_rows(groups[n_in], head_ref, h_nb)
    for k, w_ref in enumerate(w_refs):
        acc = acc + jnp.dot(rows_of(groups[k]), w_ref[...], preferred_element_type=F32)
    ho_ref[...] = acc
    hno_ref[...] = _rms(acc, g_ref[...]).astype(BF16)


def proj_residual(a_list, w_list, h, g_next, unpad=None, h_pad=None):
    m, d = a_list[0].shape[0], h.shape[1]
    if unpad is None:
        tm = _pick(m, (512, 256, 128))
        grid = (m // tm,)
        m_out = m
        row_specs = lambda width: [pl.BlockSpec((tm, width), lambda i: (i, 0))]
        out_spec = pl.BlockSpec((tm, d), lambda i: (i, 0))
    else:
        assert h_pad is None
        batch, p, skip = unpad
        assert m == batch * p and p % CHUNK == 0 and skip % CHUNK == 0
        tm = _pick(p - skip, (512, 256, 128))
        tiles = (p - skip) // tm
        grid = (batch, tiles)
        m_out = batch * (p - skip)
        n_sub = tm // CHUNK
        first = lambda i, t: i * (p // CHUNK) + skip // CHUNK + t * n_sub
        row_specs = lambda width: [pl.BlockSpec((CHUNK, width), lambda i, t, k=k: (first(i, t) + k, 0))
                                   for k in range(n_sub)]
        out_spec = pl.BlockSpec((tm, d), lambda i, t: (i * tiles + t, 0))
    spec_groups = [row_specs(a.shape[1]) for a in a_list]
    extra = []
    if h_pad is None:
        spec_groups.append(row_specs(d))
    else:
        head, nb = h_pad
        spec_groups.append(_padded_row_specs(tm, d, nb))
        extra = [head]
    pieces = tuple(len(s) for s in spec_groups)
    in_specs = [spec for s in spec_groups for spec in s] + [_resident(e.shape) for e in extra]
    in_specs += [_resident(w.shape) for w in w_list] + [_resident((1, d))]
    row_args = [x for x, cnt in zip([*a_list, h], pieces) for _ in range(cnt)]
    return pl.pallas_call(
        functools.partial(_proj_res_kernel, pieces=pieces, h_nb=None if h_pad is None else h_pad[1]),
        grid=grid,
        in_specs=in_specs,
        out_specs=[out_spec, out_spec],
        out_shape=[jax.ShapeDtypeStruct((m_out, d), F32), jax.ShapeDtypeStruct((m_out, d), BF16)],
        compiler_params=_cparams(("parallel",) * len(grid)),
        name="proj_residual",
    )(*row_args, *extra, *w_list, g_next.reshape(1, d))


def _mlp_kernel(hn_ref, w1_ref, w2_ref, h_ref, g_ref, *out_refs, n_chunks, with_norm):
    hn = hn_ref[...]
    tf = w1_ref.shape[1] // n_chunks
    acc = h_ref[...]
    for c in range(n_chunks):
        a = jnp.dot(hn, w1_ref[:, c * tf:(c + 1) * tf], preferred_element_type=F32)
        a = jnp.square(jnp.maximum(a, 0.0)).astype(BF16)
        acc = acc + jnp.dot(a, w2_ref[c * tf:(c + 1) * tf, :], preferred_element_type=F32)
    out_refs[0][...] = acc
    if with_norm:
        out_refs[1][...] = _rms(acc, g_ref[...]).astype(BF16)


def mlp_residual(hn, w1, w2, h, g_next):
    m, d = h.shape
    ff = w1.shape[1]
    tm = _pick(m, (512, 256, 128))
    with_norm = g_next is not None
    g = (g_next if with_norm else jnp.ones((d,), F32)).reshape(1, d)
    row_spec = pl.BlockSpec((tm, d), lambda i: (i, 0))
    out_specs = [row_spec]
    out_shape = [jax.ShapeDtypeStruct((m, d), F32)]
    if with_norm:
        out_specs.append(row_spec)
        out_shape.append(jax.ShapeDtypeStruct((m, d), BF16))
    res = pl.pallas_call(
        functools.partial(_mlp_kernel, n_chunks=ff // _pick(ff, (512,)), with_norm=with_norm),
        grid=(m // tm,),
        in_specs=[row_spec, _resident(w1.shape), _resident(w2.shape), row_spec, _resident((1, d))],
        out_specs=out_specs,
        out_shape=out_shape,
        compiler_params=_cparams(("parallel",)),
        name="mlp",
    )(hn, w1, w2, h, g)
    return res if with_norm else (res[0], None)


def _swap_halves(x):
    half = x.shape[-1] // 2
    return jnp.concatenate([x[:, half:], x[:, :half]], axis=-1)


def _retention_kernel(q_ref, k_ref, v_ref, g_ref, cos_ref, sin_ref, dec_ref, qd_ref, kd_ref,
                      gn_ref, o_ref, s_ref):
    @pl.when(pl.program_id(1) == 0)
    def _():
        s_ref[...] = jnp.zeros_like(s_ref)

    def body(n, carry):
        rows = pl.ds(pl.multiple_of(n * CHUNK, CHUNK), CHUNK)
        cos = cos_ref[rows, :]
        sin = sin_ref[rows, :]
        for h in range(RET_HEADS):
            qk_cols = slice(h * RET_QK_DIM, (h + 1) * RET_QK_DIM)
            v_cols = slice(h * RET_V_DIM, (h + 1) * RET_V_DIM)
            qd = qd_ref[h]
            q = q_ref[0, rows, qk_cols].astype(F32)
            k = k_ref[0, rows, qk_cols].astype(F32)
            v = v_ref[0, rows, v_cols]
            qr = q * cos + _swap_halves(q) * sin
            kr = (k * cos + _swap_halves(k) * sin) * (RET_QK_DIM ** -0.5)
            scores = lax.dot_general(qr.astype(BF16), kr.astype(BF16), (((1,), (1,)), ((), ())),
                                     preferred_element_type=F32) * dec_ref[h]
            o = jnp.dot(scores.astype(BF16), v, preferred_element_type=F32)
            state = s_ref[h]
            o = o + jnp.dot((qr * qd).astype(BF16), state.astype(BF16), preferred_element_type=F32)
            kv = lax.dot_general((kr * kd_ref[h]).astype(BF16), v, (((0,), (0,)), ((), ())),
                                 preferred_element_type=F32)
            s_ref[h] = qd[CHUNK - 1:CHUNK, 0:1] * state + kv
            mu = jnp.mean(o, axis=-1, keepdims=True)
            oc = o - mu
            var = jnp.mean(oc * oc, axis=-1, keepdims=True)
            on = oc * lax.rsqrt(var + EPS) * gn_ref[h]
            gate = g_ref[0, rows, v_cols].astype(F32)
            o_ref[0, rows, v_cols] = (gate * jax.nn.sigmoid(gate) * on).astype(o_ref.dtype)
        return carry

    lax.fori_loop(0, q_ref.shape[1] // CHUNK, body, 0, unroll=2)


def _retention_tables(p):
    half = RET_QK_DIM // 2
    inv_freq = ROPE_BASE ** (-np.arange(half, dtype=np.float64) / half)
    ang = np.arange(p, dtype=np.float64)[:, None] * inv_freq[None, :]
    cos, sin = np.cos(ang), np.sin(ang)
    cos_t = np.concatenate([cos, cos], axis=1)
    sin_t = np.concatenate([-sin, sin], axis=1)
    log_g = np.log1p(-np.exp2(-RET_DECAY_OFFSET - np.arange(RET_HEADS, dtype=np.float64)))
    idx = np.arange(CHUNK, dtype=np.float64)
    diff = idx[:, None] - idx[None, :]
    dec = np.where(diff[None] >= 0, np.exp(np.maximum(diff, 0.0)[None] * log_g[:, None, None]), 0.0)
    qd = np.exp((idx + 1.0)[None, :] * log_g[:, None])
    kd = np.exp((CHUNK - 1 - idx)[None, :] * log_g[:, None])
    bc = lambda t: np.broadcast_to(t[:, :, None], (RET_HEADS, CHUNK, RET_QK_DIM))
    f32 = lambda t: jnp.asarray(np.asarray(t, dtype=np.float32))
    return f32(cos_t), f32(sin_t), f32(dec), f32(bc(qd)), f32(bc(kd))


def retention(proj, gn_g):
    b, p, _ = proj.shape
    cos_t, sin_t, dec, qd, kd = _retention_tables(p)
    qk_w = RET_HEADS * RET_QK_DIM
    v_w = RET_HEADS * RET_V_DIM
    tt = CHUNK * _pick(p // CHUNK, (11, 3, 1))
    return pl.pallas_call(
        _retention_kernel,
        grid=(b, p // tt),
        in_specs=[pl.BlockSpec((1, tt, qk_w), lambda i, t: (i, t, 0)),
                  pl.BlockSpec((1, tt, qk_w), lambda i, t: (i, t, 1)),
                  pl.BlockSpec((1, tt, v_w), lambda i, t: (i, t, (2 * qk_w) // v_w)),
                  pl.BlockSpec((1, tt, v_w), lambda i, t: (i, t, (2 * qk_w) // v_w + 1)),
                  pl.BlockSpec((tt, RET_QK_DIM), lambda i, t: (t, 0)),
                  pl.BlockSpec((tt, RET_QK_DIM), lambda i, t: (t, 0)),
                  _resident(dec.shape), _resident(qd.shape), _resident(kd.shape),
                  _resident((RET_HEADS, 1, RET_V_DIM))],
        out_specs=pl.BlockSpec((1, tt, v_w), lambda i, t: (i, t, 0)),
        out_shape=jax.ShapeDtypeStruct((b, p, v_w), BF16),
        scratch_shapes=[pltpu.VMEM((RET_HEADS, RET_QK_DIM, RET_V_DIM), F32)],
        compiler_params=_cparams(("parallel", "arbitrary")),
        name="retention",
    )(proj, proj, proj, proj, cos_t, sin_t, dec, qd, kd,
      gn_g.astype(F32).reshape(RET_HEADS, 1, RET_V_DIM))


CONV_ROWS = 64


def _conv_kernel(a_ref, gate_ref, w_ref, b_ref, lg_ref, lb_ref, o_ref, sh_ref, y_ref):
    t = pl.program_id(1)
    tt = a_ref.shape[1]
    ch = a_ref.shape[2]
    n_sh = tt + CONV_HALO - SUBLANES

    @pl.when(t == 0)
    def _():
        sh_ref[0, 0:CONV_HALO, :] = jnp.zeros((CONV_HALO, ch), F32)

    @pl.when(t > 0)
    def _():
        sh_ref[0, 0:CONV_HALO, :] = sh_ref[0, tt:tt + CONV_HALO, :]

    gate = gate_ref[0].astype(F32)
    sh_ref[0, CONV_HALO:CONV_HALO + tt, :] = a_ref[0].astype(F32) * jax.nn.sigmoid(gate)
    for r in range(1, SUBLANES):
        sh_ref[r, 0:n_sh, :] = sh_ref[0, r:r + n_sh, :]

    first = CONV_HALO - (CONV_WIDTH - 1)

    n_rb = tt // CONV_ROWS
    for c in range(ch // 128):
        cols = slice(c * 128, (c + 1) * 128)
        ws = [jnp.broadcast_to(w_ref[j:j + 1, cols], (SUBLANES, 128)) for j in range(CONV_WIDTH)]
        bias = jnp.broadcast_to(b_ref[:, cols], (SUBLANES, 128))

        def row_block(rb, carry, cols=cols, ws=ws, bias=bias):
            base = pl.multiple_of(rb * CONV_ROWS, CONV_ROWS)
            accs = [bias] * (CONV_ROWS // SUBLANES)
            for j in range(CONV_WIDTH):
                blk, r = divmod(first + j, SUBLANES)
                for k in range(len(accs)):
                    x = sh_ref[r, pl.ds(base + (blk + k) * SUBLANES, SUBLANES), cols]
                    accs[k] = accs[k] + ws[j] * x
            for k, acc in enumerate(accs):
                y_ref[pl.ds(base + k * SUBLANES, SUBLANES), cols] = acc
            return carry

        lax.fori_loop(0, n_rb, row_block, 0, unroll=2)

    for rb in range(n_rb):
        rows = slice(rb * CONV_ROWS, (rb + 1) * CONV_ROWS)
        y = y_ref[rows, :]
        mu = jnp.mean(y, axis=-1, keepdims=True)
        yc = y - mu
        var = jnp.mean(yc * yc, axis=-1, keepdims=True)
        z = yc * lax.rsqrt(var + EPS) * lg_ref[...] + lb_ref[...]
        o_ref[0, rows, :] = (z * jax.nn.sigmoid(z)).astype(o_ref.dtype)


def conformer_conv(proj, conv_w, conv_b, ln_g, ln_b):
    b, p, n = proj.shape
    ch = D_MODEL
    tt = _pick(p, (384, 128))
    a_blk = (n - 2 * ch) // ch
    row = lambda x: x.astype(F32).reshape(1, ch)
    return pl.pallas_call(
        _conv_kernel,
        grid=(b, p // tt),
        in_specs=[pl.BlockSpec((1, tt, ch), lambda i, t: (i, t, a_blk)),
                  pl.BlockSpec((1, tt, ch), lambda i, t: (i, t, a_blk + 1)),
                  _resident((CONV_WIDTH, ch)), _resident((1, ch)), _resident((1, ch)), _resident((1, ch))],
        out_specs=pl.BlockSpec((1, tt, ch), lambda i, t: (i, t, 0)),
        out_shape=jax.ShapeDtypeStruct((b, p, ch), BF16),
        scratch_shapes=[pltpu.VMEM((SUBLANES, CONV_HALO + tt, ch), F32),
                        pltpu.VMEM((tt, ch), F32)],
        compiler_params=_cparams(("parallel", "arbitrary")),
        name="conformer_conv",
    )(proj, proj, conv_w.astype(F32), row(conv_b), row(ln_g), row(ln_b))


SB_WINDOW = 3
SB_GROUP = 5
LOG2E = 1.4426950408889634


def _softplus(z):
    return jnp.maximum(z, 0.0) + jnp.log(1.0 + jnp.exp2(jnp.abs(z) * -LOG2E))


def _split_bf16(x):
    hi = x.astype(BF16)
    return hi, (x - hi.astype(F32)).astype(BF16)


def _sb_first_group(n_blocks):
    n_full = max(n_blocks - SB_WINDOW, 0) // SB_GROUP
    return n_blocks - n_full * SB_GROUP


def _sb_kernel(q_ref, k_ref, v_ref, u_ref, o_ref, kt_ref, hl_ref, w_ref, acc_ref, csum_ref):
    n_blocks = q_ref.shape[1] // CHUNK
    n_first = _sb_first_group(n_blocks)
    lane = lax.broadcasted_iota(jnp.int32, (CHUNK, CHUNK), 1)
    rowi = lax.broadcasted_iota(jnp.int32, (CHUNK, CHUNK), 0)
    head0 = lane < SB_HEAD_DIM
    tri = lane < rowi
    lane2 = lax.broadcasted_iota(jnp.int32, (2 * CHUNK, CHUNK), 1)
    row2 = lax.broadcasted_iota(jnp.int32, (2 * CHUNK, CHUNK), 0) & (CHUNK - 1)

    def rows(j):
        if isinstance(j, int):
            return slice(j * CHUNK, (j + 1) * CHUNK)
        return pl.ds(pl.multiple_of(j * CHUNK, CHUNK), CHUNK)

    for j in range(n_blocks):
        kt_ref[j] = k_ref[0, rows(j), :].astype(F32).T.astype(BF16)

    def stacked_q(i):
        q = q_ref[0, rows(i), :]
        zero = jnp.zeros_like(q)
        return jnp.concatenate([jnp.where(head0, q, zero), jnp.where(head0, zero, q)], axis=0)

    def tile(x, a, blk):
        return x[a * CHUNK:(a + 1) * CHUNK, blk * CHUNK:(blk + 1) * CHUNK]

    def slot_rows(u, a, blk):
        r0 = ((u * SB_WINDOW + blk) * 2 + a) * CHUNK
        return slice(r0, r0 + CHUNK)

    def masked_block(u, qs, i, j):
        key_pos = j * CHUNK + lane2
        valid = (key_pos < i * CHUNK + row2) & (key_pos >= PAD_FRONT)
        z = jnp.dot(qs, kt_ref[j], preferred_element_type=F32)
        sp = jnp.where(valid, _softplus(z), 0.0)
        hi, lo = _split_bf16(sp)
        sums = jnp.dot(jnp.concatenate([hi, lo], axis=1), u_ref[...], preferred_element_type=F32)
        csum = csum_ref[u]
        w = jnp.where(valid, jnp.exp(z - sums[:, :CHUNK] - csum), 0.0)
        acc_ref[u] += jnp.dot(w.astype(BF16), v_ref[0, rows(j), :], preferred_element_type=F32)
        csum_ref[u] = csum + sums[:, CHUNK:]

    def window_scores(u, qs, i, masked):
        js = [i - (SB_WINDOW - 1) + blk for blk in range(SB_WINDOW)]
        jc = [max(j, 0) for j in js] if masked else js
        kt = jnp.concatenate([kt_ref[j] for j in jc], axis=1)
        zs = jnp.dot(qs, kt, preferred_element_type=F32)
        valids = []
        for blk in range(SB_WINDOW):
            if masked:
                key_pos = js[blk] * CHUNK + lane
                valid = (key_pos < i * CHUNK + rowi) & (key_pos >= PAD_FRONT)
            else:
                valid = tri if blk == SB_WINDOW - 1 else None
            valids.append(valid)
            for a in range(2):
                sp = _softplus(tile(zs, a, blk))
                if valid is not None:
                    sp = jnp.where(valid, sp, 0.0)
                hi, lo = _split_bf16(sp)
                hl_ref[slot_rows(u, a, blk), 0:CHUNK] = hi
                hl_ref[slot_rows(u, a, blk), CHUNK:2 * CHUNK] = lo
        return zs, valids, jc

    def window_weights(u, zs, valids, jc, sums):
        for a in range(2):
            csum = None
            for blk in reversed(range(SB_WINDOW)):
                x = tile(zs, a, blk) - sums[slot_rows(u, a, blk), :CHUNK]
                if csum is not None:
                    x = x - csum
                w = jnp.exp(x)
                if valids[blk] is not None:
                    w = jnp.where(valids[blk], w, 0.0)
                w_ref[u, a * CHUNK:(a + 1) * CHUNK, blk * CHUNK:(blk + 1) * CHUNK] = w.astype(BF16)
                tot = sums[slot_rows(u, a, blk), CHUNK:]
                csum = tot if csum is None else csum + tot
            csum_ref[u, a * CHUNK:(a + 1) * CHUNK, :] = csum
        v_cat = jnp.concatenate([v_ref[0, rows(j), :] for j in jc], axis=0)
        acc_ref[u] = jnp.dot(w_ref[u], v_cat, preferred_element_type=F32)

    def finish(u, qs, i, j_start):
        def cond(j):
            return jnp.logical_and(j >= 0, jnp.min(csum_ref[u]) <= -SB_EXIT_LOG)

        def body(j):
            masked_block(u, qs, i, j)
            return j - 1

        lax.while_loop(cond, body, j_start)

    def write_out(u, i):
        acc = acc_ref[u]
        o_ref[0, rows(i), :] = jnp.where(head0, acc[:CHUNK], acc[CHUNK:]).astype(o_ref.dtype)

    def group(i0, n, masked):
        qss = [stacked_q(i0 + u) for u in range(n)]
        scored = [window_scores(u, qss[u], i0 + u, masked) for u in range(n)]
        sums = jnp.dot(hl_ref[0:n * SB_WINDOW * 2 * CHUNK, :], u_ref[...], preferred_element_type=F32)
        for u in range(n):
            window_weights(u, *scored[u], sums)
        low = csum_ref[0]
        for u in range(1, n):
            low = jnp.minimum(low, csum_ref[u])

        @pl.when(jnp.min(low) <= -SB_EXIT_LOG)
        def _():
            for u in range(n):
                finish(u, qss[u], i0 + u, i0 + u - SB_WINDOW)

        for u in range(n):
            write_out(u, i0 + u)

    def group_body(g, c):
        group(n_first + g * SB_GROUP, SB_GROUP, False)
        return c

    group(0, n_first, True)
    lax.fori_loop(0, (n_blocks - n_first) // SB_GROUP, group_body, 0)


def _sb_sum_matrix():
    s = np.arange(CHUNK)
    incl = (s[:, None] >= s[None, :]).astype(np.float32)
    half = np.concatenate([incl, np.ones((CHUNK, CHUNK), np.float32)], axis=1)
    return jnp.asarray(np.concatenate([half, half], axis=0), dtype=BF16)


def stick_breaking(qkv):
    _, b, p, _ = qkv.shape
    npair = D_MODEL // 128
    n_slots = max(SB_GROUP, _sb_first_group(p // CHUNK))
    slot = (n_slots, 2 * CHUNK, CHUNK)
    return pl.pallas_call(
        _sb_kernel,
        grid=(b, npair),
        in_specs=[pl.BlockSpec((None, 1, p, 128), lambda i, h: (h, i, 0, 0)),
                  pl.BlockSpec((None, 1, p, 128), lambda i, h: (npair + h, i, 0, 0)),
                  pl.BlockSpec((None, 1, p, 128), lambda i, h: (2 * npair + h, i, 0, 0)),
                  _resident((2 * CHUNK, 2 * CHUNK))],
        out_specs=pl.BlockSpec((1, p, 128), lambda i, h: (i, 0, h)),
        out_shape=jax.ShapeDtypeStruct((b, p, D_MODEL), BF16),
        scratch_shapes=[pltpu.VMEM((p // CHUNK, CHUNK, CHUNK), BF16),
                        pltpu.VMEM((n_slots * SB_WINDOW * 2 * CHUNK, 2 * CHUNK), BF16),
                        pltpu.VMEM((n_slots, 2 * CHUNK, SB_WINDOW * CHUNK), BF16),
                        pltpu.VMEM(slot, F32), pltpu.VMEM(slot, F32)],
        compiler_params=_cparams(("parallel", "parallel")),
        name="stick_breaking",
    )(qkv, qkv, qkv, _sb_sum_matrix())


def kernel(x, meta, norm_mix_g, norm_mlp_g, even_w_in, even_ret_gn_g, even_conv_w, even_conv_b,
           even_conv_ln_g, even_conv_ln_b, even_w_out, odd_w_qkv, odd_q_norm_g, odd_k_norm_g,
           odd_w_o, mlp_w1, mlp_w2):
    b, seq, d = x.shape
    depth = norm_mix_g.shape[0]
    p = PAD_FRONT + N_META + seq
    assert d == D_MODEL and p % CHUNK == 0
    assert depth == 2
    m = b * p
    nb = p // CHUNK
    head = jnp.concatenate([jnp.zeros((PAD_FRONT, d), x.dtype), meta.astype(x.dtype)], axis=0)
    h = x.reshape(b * seq, d)
    hn = None
    for layer in range(depth):
        j = layer // 2
        last = layer + 1 == depth
        unpad = (b, p, PAD_FRONT + N_META) if last else None
        if layer % 2 == 0:
            assert layer == 0
            proj = norm_matmul(h, head, norm_mix_g[layer], even_w_in[j].astype(BF16), m, nb).reshape(b, p, -1)
            o = retention(proj, even_ret_gn_g[j])
            c = conformer_conv(proj, even_conv_w[j], even_conv_b[j], even_conv_ln_g[j], even_conv_ln_b[j])
            w_out = even_w_out[j].astype(BF16)
            n_o = o.shape[-1]
            h, hn = proj_residual([o.reshape(m, -1), c.reshape(m, -1)], [w_out[:n_o], w_out[n_o:]],
                                  h, norm_mlp_g[layer], unpad, h_pad=(head, nb))
        else:
            qkv = qkv_proj(hn, odd_w_qkv[j].astype(BF16), odd_q_norm_g[j], odd_k_norm_g[j])
            o = stick_breaking(qkv.reshape(-1, b, p, 128))
            h, hn = proj_residual([o.reshape(m, -1)], [odd_w_o[j].astype(BF16)], h, norm_mlp_g[layer], unpad)
        g_next = None if last else norm_mix_g[layer + 1]
        h, hn = mlp_residual(hn, mlp_w1[layer].astype(BF16), mlp_w2[layer].astype(BF16), h, g_next)
    return h.reshape(b, seq, d)
```
